```python
import math
import jax, jax.numpy as jnp
from jax import lax
import numpy as np

D_MODEL = 1024
BATCH = 32
SEQ = 256
DEPTH = 4
DEC_BATCH = 2
DEC_SEQ = 4096
PAST_LEN = 256

GRID_W = 64
D_MIX = D_MODEL
D_GROUP = D_MIX // 4
HEAD_DIM = 64
H_A = D_GROUP // HEAD_DIM
H_D = D_GROUP // HEAD_DIM
N_DIR = 2
RANK_W = 64
RANK_A = 64
RANK_G = 128
POOL_WINDOWS = (2, 4, 8, 16)
N_POOL = len(POOL_WINDOWS)
POOL_CH = D_GROUP // N_POOL
CONV_W = 31
QK_CONV_W = 3
MLSTM_CHUNK = 64
D_FF = 4 * D_MODEL
EPS = 1e-6
RWKV_LN_EPS = 64e-5
CONV_LN_EPS = 1e-5

A_W = 3 * D_GROUP + RANK_W + RANK_A + RANK_G
B_W = D_GROUP
C_W = 2 * D_GROUP
D_W = 4 * D_GROUP + 2 * N_DIR * H_D
D_IN = A_W + B_W + C_W + D_W
IN_SPLITS = [A_W, A_W + B_W, A_W + B_W + C_W]
A_SPLITS = [D_GROUP, 2 * D_GROUP, 3 * D_GROUP, 3 * D_GROUP + RANK_W, 3 * D_GROUP + RANK_W + RANK_A]
D_SPLITS = [2 * D_GROUP, 3 * D_GROUP, 3 * D_GROUP + N_DIR * H_D, 3 * D_GROUP + 2 * N_DIR * H_D]

kernel_name = 'hybrid_rwkv7_pool_conformer_mlstm_dit_step'

F32 = jnp.float32


def rms_norm(x, g):
    xf = x.astype(F32)
    y = xf * lax.rsqrt(jnp.mean(xf * xf, axis=-1, keepdims=True) + EPS)
    return (y * g).astype(x.dtype)


def layer_norm(x, g, b, eps):
    xf = x.astype(F32)
    mu = jnp.mean(xf, axis=-1, keepdims=True)
    var = jnp.mean(jnp.square(xf - mu), axis=-1, keepdims=True)
    return ((xf - mu) * lax.rsqrt(var + eps) * g + b).astype(x.dtype)


def dwconv(x, w):
    k, ch = w.shape
    return lax.conv_general_dilated(x, w[:, None, :], window_strides=(1,), padding=[(k // 2, k // 2)],
                                    dimension_numbers=('NWC', 'WIO', 'NWC'), feature_group_count=ch)


def token_shift(x):
    zero = jnp.zeros_like(x[:, :1])
    prev = jnp.concatenate([zero, x[:, :-1]], axis=1)
    nxt = jnp.concatenate([x[:, 1:], zero], axis=1)
    return 0.5 * (prev + nxt)


def window_bounds(n, win):
    t = np.arange(n)
    return np.clip(t - win // 2, 0, n), np.clip(t + win // 2, 0, n)


def pool_seq(x, win):
    b, t, ch = x.shape
    cs = jnp.concatenate([jnp.zeros((b, 1, ch), x.dtype), jnp.cumsum(x, axis=1)], axis=1)
    lo, hi = window_bounds(t, win)
    cnt = (hi - lo).astype(np.float32)
    return (cs[:, hi] - cs[:, lo]) / cnt[None, :, None]


def pool_grid(x, win):
    b, t, ch = x.shape
    rows = t // GRID_W
    g = x.reshape(b, rows, GRID_W, ch)
    sat = jnp.pad(jnp.cumsum(jnp.cumsum(g, axis=1), axis=2), ((0, 0), (1, 0), (1, 0), (0, 0)))
    r_lo, r_hi = window_bounds(rows, win)
    c_lo, c_hi = window_bounds(GRID_W, win)
    band = sat[:, r_hi] - sat[:, r_lo]
    box = band[:, :, c_hi] - band[:, :, c_lo]
    cnt = np.outer(r_hi - r_lo, c_hi - c_lo).astype(np.float32)
    return (box / cnt[None, :, :, None]).reshape(b, t, ch)


def rwkv_scan(r, w, k, v, kk, a, s0, reverse):
    def step(s, inp):
        r_t, w_t, k_t, v_t, kk_t, a_t = inp
        sk = jnp.einsum('bhvk,bhk->bhv', s, kk_t)
        s = (s * w_t[:, :, None, :] - sk[..., None] * (kk_t * a_t)[:, :, None, :]
             + v_t[..., None] * k_t[:, :, None, :])
        return s, jnp.einsum('bhvk,bhk->bhv', s, r_t)
    xs = tuple(jnp.moveaxis(z, 1, 0) for z in (r, w, k, v, kk, a))
    s_final, y = lax.scan(step, s0, xs, reverse=reverse)
    return jnp.moveaxis(y, 0, 1), s_final


def rwkv_mix(z, s0, P, l):
    dt = z.dtype
    z = z + (token_shift(z) - z) * P['rwkv_mu'][l]
    r, k, v, wd, ad, gd = jnp.split(z, A_SPLITS, axis=-1)
    b, t, _ = r.shape
    hd = lambda u: u.astype(F32).reshape(b, t, H_A, HEAD_DIM)
    kk = hd(k * P['rwkv_k_k'][l])
    kk = kk * lax.rsqrt(jnp.sum(kk * kk, axis=-1, keepdims=True) + 1e-12)
    rh, vh = hd(r), hd(v)
    wt = jnp.tanh(wd)
    g = jax.nn.sigmoid(gd) @ P['rwkv_g_up'][l]
    ys, finals = [], []
    for d in range(N_DIR):
        w_log = -jax.nn.softplus(-(P['rwkv_w0'][l, d] + wt @ P['rwkv_w_up'][l, d])) - 0.5
        decay = jnp.exp(-jnp.exp(w_log.astype(F32)))
        a = jax.nn.sigmoid(P['rwkv_a0'][l, d] + ad @ P['rwkv_a_up'][l, d])
        kd = hd(k * (1.0 + (a - 1.0) * P['rwkv_k_a'][l]))
        yd, sd = rwkv_scan(rh, hd(decay), kd, vh, kk, hd(a), s0[:, d].astype(F32), reverse=(d == 1))
        bonus = jnp.sum(rh * kd * P['rwkv_r_k'][l], axis=-1, keepdims=True) * vh
        ys.append(yd + bonus)
        finals.append(sd)
    y = layer_norm(ys[0] + ys[1], P['rwkv_ln_g'][l].reshape(H_A, HEAD_DIM),
                   P['rwkv_ln_b'][l].reshape(H_A, HEAD_DIM), RWKV_LN_EPS)
    return (y.reshape(b, t, D_GROUP) * g).astype(dt), jnp.stack(finals, axis=1)


def pool_mix(z, P, l, grid):
    zf = z.astype(F32)
    outs = []
    for gi, win in enumerate(POOL_WINDOWS):
        zg = zf[..., gi * POOL_CH:(gi + 1) * POOL_CH]
        pooled = pool_grid(zg, win) if grid else pool_seq(zg, win)
        outs.append((pooled - zg).astype(z.dtype) @ P['pool_w'][l, gi])
    return jnp.concatenate(outs, axis=-1) * P['pool_scale'][l]


def conv_mix(z, P, l):
    val, gate = jnp.split(z, 2, axis=-1)
    u = val * jax.nn.sigmoid(gate)
    u = dwconv(u, P['conv_dw'][l]) + P['conv_b'][l]
    u = jax.nn.silu(layer_norm(u, P['conv_ln_g'][l], P['conv_ln_b'][l], CONV_LN_EPS))
    return u @ P['conv_pw'][l]


def mlstm_chunkwise(q, k, v, logi, logf, c0, n0, m0):
    b, t, h, dh = q.shape
    L = MLSTM_CHUNK
    nc = t // L
    chunks = lambda z: jnp.moveaxis(z.reshape((b, nc, L) + z.shape[2:]), 1, 0)
    causal = jnp.tril(jnp.ones((L, L), dtype=bool))[None, :, :, None]

    def step(carry, inp):
        c, n, m = carry
        qc, kc, vc, li, lf = inp
        bcum = jnp.cumsum(lf, axis=1)
        log_w = jnp.where(causal, bcum[:, :, None] - bcum[:, None] + li[:, None], -jnp.inf)
        m_t = jnp.maximum(bcum + m[:, None], jnp.max(log_w, axis=2))
        w = jnp.exp(log_w - m_t[:, :, None])
        inter = jnp.exp(bcum + m[:, None] - m_t)
        s = jnp.einsum('bthd,bshd->btsh', qc, kc) * w
        num = jnp.einsum('btsh,bshd->bthd', s, vc) + inter[..., None] * jnp.einsum('bhvk,bthk->bthv', c, qc)
        den = jnp.sum(s, axis=2) + inter * jnp.einsum('bhk,bthk->bth', n, qc)
        hc = num / jnp.maximum(jnp.abs(den), jnp.exp(-m_t))[..., None]
        m_new = m_t[:, -1]
        w_end = jnp.exp(bcum[:, -1:] - bcum + li - m_new[:, None])
        carry_decay = jnp.exp(bcum[:, -1] + m - m_new)
        c = carry_decay[..., None, None] * c + jnp.einsum('bsh,bshv,bshk->bhvk', w_end, vc, kc)
        n = carry_decay[..., None] * n + jnp.einsum('bsh,bshk->bhk', w_end, kc)
        return (c, n, m_new), hc

    (c, n, m), hs = lax.scan(step, (c0, n0, m0), tuple(chunks(z) for z in (q, k, v, logi, logf)))
    return jnp.moveaxis(hs, 0, 1).reshape(b, t, h, dh), (c, n, m)


def mlstm_mix(z, c0, n0, m0, P, l):
    dt = z.dtype
    qk, v, ipre, fpre, opre = jnp.split(z, D_SPLITS, axis=-1)
    qk = jax.nn.silu(dwconv(qk, P['mlstm_qk_conv'][l]))
    q, k = jnp.split(qk, 2, axis=-1)
    b, t, _ = q.shape
    hd = lambda u: u.astype(F32).reshape(b, t, H_D, HEAD_DIM)
    logi = (ipre + P['mlstm_i_bias'][l]).astype(F32).reshape(b, t, N_DIR, H_D)
    logf = jax.nn.log_sigmoid((fpre + P['mlstm_f_bias'][l]).astype(F32)).reshape(b, t, N_DIR, H_D)
    qh, kh, vh = hd(q), hd(k) * (1.0 / math.sqrt(HEAD_DIM)), hd(v)
    hs, cs, ns, ms = [], [], [], []
    for d in range(N_DIR):
        seqs = (qh, kh, vh, logi[:, :, d], logf[:, :, d])
        if d == 1:
            seqs = tuple(jnp.flip(u, axis=1) for u in seqs)
        hdir, (cd, nd, md) = mlstm_chunkwise(*seqs, c0[:, d].astype(F32), n0[:, d].astype(F32), m0[:, d].astype(F32))
        if d == 1:
            hdir = jnp.flip(hdir, axis=1)
        hs.append(hdir)
        cs.append(cd)
        ns.append(nd)
        ms.append(md)
    hsum = (hs[0] + hs[1]) * jax.nn.sigmoid(hd(opre))
    hsum = rms_norm(hsum, P['mlstm_hn_g'][l].reshape(H_D, HEAD_DIM)).reshape(b, t, D_GROUP)
    return hsum.astype(dt), (jnp.stack(cs, axis=1), jnp.stack(ns, axis=1), jnp.stack(ms, axis=1))


def trunk_layer(x, mod, P, l, states, grid):
    sh1, sc1, g1, sh2, sc2, g2 = jnp.split(mod[:, None, :], 6, axis=-1)
    h = rms_norm(x, P['norm1_g'][l]) * (1.0 + sc1) + sh1
    z = h @ P['w_in'][l]
    za, zb, zc, zd = jnp.split(z, IN_SPLITS, axis=-1)
    s_rwkv, s_c, s_n, s_m = states
    ya, f_rwkv = rwkv_mix(za, s_rwkv, P, l)
    yb = pool_mix(zb, P, l, grid)
    yc = conv_mix(zc, P, l)
    yd, (f_c, f_n, f_m) = mlstm_mix(zd, s_c, s_n, s_m, P, l)
    y = jnp.concatenate([ya, yb, yc, yd], axis=-1) @ P['w_out'][l]
    x = x + g1 * y
    h = rms_norm(x, P['norm2_g'][l]) * (1.0 + sc2) + sh2
    f = jnp.square(jax.nn.relu(h @ P['mlp_w1'][l] + P['mlp_b1'][l])) @ P['mlp_w2'][l] + P['mlp_b2'][l]
    x = x + g2 * f
    return x, (f_rwkv, f_c, f_n, f_m)


def setup_inputs(seed: int = 0) -> dict:
    key = jax.random.key(seed)
    ks = iter(jax.random.split(key, 64))
    nrm = lambda shape, s: s * jax.random.normal(next(ks), shape, F32)
    unif = lambda shape, lo, hi: jax.random.uniform(next(ks), shape, F32, lo, hi)
    gain = lambda shape: 1.0 + nrm(shape, 0.01)
    L, D = DEPTH, D_MODEL
    return {
        'x_prompt': nrm((BATCH, SEQ, D), 1.0),
        'x_sample': nrm((DEC_BATCH, DEC_SEQ, D), 1.0),
        'c': nrm((DEC_BATCH, D), 1.0),
        'state_rwkv': nrm((DEC_BATCH, L, N_DIR, H_A, HEAD_DIM, HEAD_DIM), 0.1),
        'state_mlstm_C': nrm((DEC_BATCH, L, N_DIR, H_D, HEAD_DIM, HEAD_DIM), 0.1),
        'state_mlstm_n': nrm((DEC_BATCH, L, N_DIR, H_D, HEAD_DIM), 0.1),
        'state_mlstm_m': nrm((DEC_BATCH, L, N_DIR, H_D), 1.0),
        'c_ctx': nrm((D,), 1.0),
        'w_mod': nrm((L, D, 6 * D), 0.5 * D ** -0.5),
        'b_mod': nrm((L, 6 * D), 0.01),
        'norm1_g': gain((L, D)),
        'norm2_g': gain((L, D)),
        'w_in': nrm((L, D, D_IN), D ** -0.5),
        'w_out': nrm((L, D_MIX, D), D_MIX ** -0.5),
        'rwkv_mu': unif((L, A_W), 0.0, 1.0),
        'rwkv_w0': unif((L, N_DIR, D_GROUP), -6.0, 1.0),
        'rwkv_w_up': nrm((L, N_DIR, RANK_W, D_GROUP), 0.1 * RANK_W ** -0.5),
        'rwkv_a0': nrm((L, N_DIR, D_GROUP), 0.1),
        'rwkv_a_up': nrm((L, N_DIR, RANK_A, D_GROUP), 0.1 * RANK_A ** -0.5),
        'rwkv_g_up': nrm((L, RANK_G, D_GROUP), RANK_G ** -0.5),
        'rwkv_k_k': 0.85 + nrm((L, D_GROUP), 0.02),
        'rwkv_k_a': gain((L, D_GROUP)),
        'rwkv_r_k': nrm((L, H_A, HEAD_DIM), 0.1),
        'rwkv_ln_g': gain((L, D_GROUP)),
        'rwkv_ln_b': nrm((L, D_GROUP), 0.01),
        'pool_w': nrm((L, N_POOL, POOL_CH, POOL_CH), POOL_CH ** -0.5),
        'pool_scale': gain((L, D_GROUP)),
        'conv_dw': nrm((L, CONV_W, D_GROUP), CONV_W ** -0.5),
        'conv_b': nrm((L, D_GROUP), 0.01),
        'conv_ln_g': gain((L, D_GROUP)),
        'conv_ln_b': nrm((L, D_GROUP), 0.01),
        'conv_pw': nrm((L, D_GROUP, D_GROUP), D_GROUP ** -0.5),
        'mlstm_qk_conv': nrm((L, QK_CONV_W, 2 * D_GROUP), QK_CONV_W ** -0.5),
        'mlstm_i_bias': nrm((L, N_DIR * H_D), 0.1),
        'mlstm_f_bias': jnp.linspace(3.0, 6.0, N_DIR * H_D, dtype=F32)[None, :] + nrm((L, N_DIR * H_D), 0.1),
        'mlstm_hn_g': gain((L, D_GROUP)),
        'mlp_w1': nrm((L, D, D_FF), D ** -0.5),
        'mlp_b1': nrm((L, D_FF), 0.01),
        'mlp_w2': nrm((L, D_FF, D), D_FF ** -0.5),
        'mlp_b2': nrm((L, D), 0.01),
        'final_g': gain((D,)),
    }


def reference(x_prompt, x_sample, c, state_rwkv, state_mlstm_C, state_mlstm_n, state_mlstm_m, c_ctx,
              w_mod, b_mod, norm1_g, norm2_g, w_in, w_out,
              rwkv_mu, rwkv_w0, rwkv_w_up, rwkv_a0, rwkv_a_up, rwkv_g_up, rwkv_k_k, rwkv_k_a, rwkv_r_k,
              rwkv_ln_g, rwkv_ln_b, pool_w, pool_scale, conv_dw, conv_b, conv_ln_g, conv_ln_b, conv_pw,
              mlstm_qk_conv, mlstm_i_bias, mlstm_f_bias, mlstm_hn_g,
              mlp_w1, mlp_b1, mlp_w2, mlp_b2, final_g):
    P = dict(norm1_g=norm1_g, norm2_g=norm2_g, w_in=w_in, w_out=w_out,
             rwkv_mu=rwkv_mu, rwkv_w0=rwkv_w0, rwkv_w_up=rwkv_w_up, rwkv_a0=rwkv_a0, rwkv_a_up=rwkv_a_up,
             rwkv_g_up=rwkv_g_up, rwkv_k_k=rwkv_k_k, rwkv_k_a=rwkv_k_a, rwkv_r_k=rwkv_r_k,
             rwkv_ln_g=rwkv_ln_g, rwkv_ln_b=rwkv_ln_b, pool_w=pool_w, pool_scale=pool_scale,
             conv_dw=conv_dw, conv_b=conv_b, conv_ln_g=conv_ln_g, conv_ln_b=conv_ln_b, conv_pw=conv_pw,
             mlstm_qk_conv=mlstm_qk_conv, mlstm_i_bias=mlstm_i_bias, mlstm_f_bias=mlstm_f_bias,
             mlstm_hn_g=mlstm_hn_g, mlp_w1=mlp_w1, mlp_b1=mlp_b1, mlp_w2=mlp_w2, mlp_b2=mlp_b2)
    bp = x_prompt.shape[0]
    zero_states = (jnp.zeros((bp, N_DIR, H_A, HEAD_DIM, HEAD_DIM), F32),
                   jnp.zeros((bp, N_DIR, H_D, HEAD_DIM, HEAD_DIM), F32),
                   jnp.zeros((bp, N_DIR, H_D, HEAD_DIM), F32),
                   jnp.zeros((bp, N_DIR, H_D), F32))
    xp, xs = x_prompt, x_sample
    new_r, new_c, new_n, new_m = [], [], [], []
    for l in range(DEPTH):
        mod_ctx = (jax.nn.silu(c_ctx) @ w_mod[l] + b_mod[l])[None, :]
        xp, (sr, sc, sn, sm) = trunk_layer(xp, mod_ctx, P, l, zero_states, grid=False)
        new_r.append(sr)
        new_c.append(sc)
        new_n.append(sn)
        new_m.append(sm)
        mod_lat = jax.nn.silu(c) @ w_mod[l] + b_mod[l]
        cached = (state_rwkv[:, l], state_mlstm_C[:, l], state_mlstm_n[:, l], state_mlstm_m[:, l])
        xs, _ = trunk_layer(xs, mod_lat, P, l, cached, grid=True)
    y_prompt = rms_norm(xp, final_g)
    y_sample = rms_norm(xs, final_g)
    dt = x_prompt.dtype
    return (y_prompt, y_sample,
            jnp.stack(new_r, axis=1).astype(dt), jnp.stack(new_c, axis=1).astype(dt),
            jnp.stack(new_n, axis=1).astype(dt), jnp.stack(new_m, axis=1).astype(dt))
```

```python
import functools
import math

import jax
import jax.numpy as jnp
from jax import lax
from jax.experimental import pallas as pl
from jax.experimental.pallas import tpu as pltpu

F32 = jnp.float32
BF16 = jnp.bfloat16

D_MODEL = 1024
D_GROUP = 256
HEAD_DIM = 64
N_HEADS = 4
N_DIR = 2
DEPTH = 4
GRID_W = 64
POOL_HALF = (1, 2, 4, 8)
MAX_HALF = POOL_HALF[-1]
CONV_W = 31
CONV_PAD = 16
D_FF = 4 * D_MODEL
EPS = 1e-6
RWKV_LN_EPS = 64e-5
CONV_LN_EPS = 1e-5
CHUNK = 64
ROW_TILE = 256
SUB = 8
VMEM_LIMIT = 56 * 1024 * 1024
HI = lax.Precision.HIGHEST

Z_W = 2944
ZA_W, ZD_QK_W, ZC_W, ZB_W, ZG_W = 1024, 512, 512, 256, 128
ZD_OFF, ZC_OFF, ZB_OFF, ZG_OFF = 1024, 2048, 2560, 2816


def _mm(a, b):
    return jnp.dot(a.astype(BF16), b.astype(BF16), preferred_element_type=F32)


def _mm_nt(a, b, precision=None):
    if precision is None:
        a, b = a.astype(BF16), b.astype(BF16)
    return lax.dot_general(a, b, (((1,), (1,)), ((), ())), precision=precision, preferred_element_type=F32)


def _mm_tn(a, b, precision=None):
    if precision is None:
        a, b = a.astype(BF16), b.astype(BF16)
    return lax.dot_general(a, b, (((0,), (0,)), ((), ())), precision=precision, preferred_element_type=F32)


def _mm_hi(a, b):
    return jnp.dot(a, b, precision=HI, preferred_element_type=F32)


def _tri(n, reverse, strict):
    t = lax.broadcasted_iota(jnp.int32, (n, n), 0)
    s = lax.broadcasted_iota(jnp.int32, (n, n), 1)
    if reverse:
        return (s > t) if strict else (s >= t)
    return (s < t) if strict else (s <= t)


def _head_ones():
    a = lax.broadcasted_iota(jnp.int32, (D_GROUP, D_GROUP), 0) // HEAD_DIM
    b = lax.broadcasted_iota(jnp.int32, (D_GROUP, D_GROUP), 1) // HEAD_DIM
    return (a == b).astype(F32)


def _sigmoid(x):
    return jax.nn.sigmoid(x)


def _softplus(x):
    return jnp.maximum(x, 0.0) + jnp.log1p(jnp.exp(-jnp.abs(x)))


def _rms(x, g):
    return x * lax.rsqrt(jnp.mean(x * x, axis=-1, keepdims=True) + EPS) * g


def _params(*sem):
    return pltpu.CompilerParams(dimension_semantics=sem, vmem_limit_bytes=VMEM_LIMIT)


def _const(shape):
    return pl.BlockSpec(shape, lambda *_: (0,) * len(shape))


def _neighbours(z, prev_blk, next_blk, i, tt, t):
    row = lax.broadcasted_iota(jnp.int32, z.shape, 0)
    at_start = (i * tt) % t == 0
    at_end = ((i + 1) * tt) % t == 0
    first = jnp.where(at_start, 0.0, prev_blk[SUB - 1:SUB, :])
    last = jnp.where(at_end, 0.0, next_blk[0:1, :])
    prev = jnp.where(row == 0, first, pltpu.roll(z, 1, axis=0))
    nxt = jnp.where(row == tt - 1, last, pltpu.roll(z, tt - 1, axis=0))
    return prev, nxt


def _halo_specs(width, col, n_rows, tt):
    per = tt // SUB
    last = n_rows // SUB - 1
    prev = pl.BlockSpec((SUB, width), lambda i: (jnp.maximum(i * per - 1, 0), col))
    nxt = pl.BlockSpec((SUB, width), lambda i: (jnp.minimum((i + 1) * per, last), col))
    return prev, nxt


def _mod_kernel(c_ref, w_ref, b_ref, o_ref):
    c = c_ref[...]
    o_ref[0] = _mm(c * _sigmoid(c), w_ref[0]) + b_ref[0]


def modulation(cvec, w_mod, b_mod):
    tn = 1536
    n = w_mod.shape[-1]
    return pl.pallas_call(
        _mod_kernel,
        grid=(DEPTH, n // tn),
        in_specs=[pl.BlockSpec((SUB, D_MODEL), lambda l, j: (0, 0)),
                  pl.BlockSpec((1, D_MODEL, tn), lambda l, j: (l, 0, j)),
                  pl.BlockSpec((1, 1, tn), lambda l, j: (l, 0, j))],
        out_specs=pl.BlockSpec((1, SUB, tn), lambda l, j: (l, 0, j)),
        out_shape=jax.ShapeDtypeStruct((DEPTH, SUB, n), F32),
        compiler_params=_params("parallel", "parallel"),
        name="modulation",
    )(cvec, w_mod, b_mod.reshape(DEPTH, 1, n))


def _mod_spec(per_batch, rows_per_seq):
    if per_batch:
        return pl.BlockSpec((1, 6, D_MODEL), lambda i: (i * ROW_TILE // rows_per_seq, 0, 0))
    return pl.BlockSpec((1, 6, D_MODEL), lambda i: (0, 0, 0))


def _in_proj_kernel(x_ref, mod_ref, g_ref, w_ref, z_ref):
    h = _rms(x_ref[...], g_ref[...]) * (1.0 + mod_ref[0, 1:2, :]) + mod_ref[0, 0:1, :]
    z_ref[...] = jnp.dot(h.astype(BF16), w_ref[...], preferred_element_type=F32)


def in_proj(x, mod, g, w, t):
    n = x.shape[0]
    return pl.pallas_call(
        _in_proj_kernel,
        grid=(n // ROW_TILE,),
        in_specs=[pl.BlockSpec((ROW_TILE, D_MODEL), lambda i: (i, 0)),
                  _mod_spec(mod.shape[0] > 1, t),
                  _const((1, D_MODEL)),
                  _const((D_MODEL, Z_W))],
        out_specs=pl.BlockSpec((ROW_TILE, Z_W), lambda i: (i, 0)),
        out_shape=jax.ShapeDtypeStruct((n, Z_W), F32),
        compiler_params=_params("parallel"),
        name="in_proj",
    )(x, mod, g, w)


def _rwkv_prep_kernel(t, z_ref, zp_ref, zn_ref, mu_ref, kk_ref, ka_ref, rk_ref, w0_ref, a0_ref,
                      wup_ref, aup_ref, gup_ref,
                      r_o, kk_o, v_o, g_o, bon_o, lw0_o, lw1_o, kd0_o, kd1_o, b0_o, b1_o):
    i = pl.program_id(0)
    z = z_ref[...]
    prev, nxt = _neighbours(z, zp_ref[...], zn_ref[...], i, ROW_TILE, t)
    zm = z + (0.5 * (prev + nxt) - z) * mu_ref[...]
    r, k, v = zm[:, 0:256], zm[:, 256:512], zm[:, 512:768]
    lora = zm[:, 768:896]
    gd = zm[:, 896:1024]
    ones = _head_ones()
    kk = k * kk_ref[...]
    kk = kk * lax.rsqrt(_mm_hi(kk * kk, ones) + 1e-12)
    wt = jnp.tanh(lora)
    r_o[...] = r
    kk_o[...] = kk
    v_o[...] = v
    g_o[...] = _mm(_sigmoid(gd), gup_ref[...])
    bon = jnp.zeros_like(r)
    for d, (lw_o, kd_o, b_o) in enumerate(((lw0_o, kd0_o, b0_o), (lw1_o, kd1_o, b1_o))):
        w_log = -_softplus(-(w0_ref[d] + _mm(wt, wup_ref[d]))) - 0.5
        lw_o[...] = -jnp.exp(w_log)
        a = _sigmoid(a0_ref[d] + _mm(lora, aup_ref[d]))
        kd = k * (1.0 + (a - 1.0) * ka_ref[...])
        kd_o[...] = kd
        b_o[...] = -a * kk
        bon = bon + _mm_hi(r * kd * rk_ref[...], ones) * v
    bon_o[...] = bon


def rwkv_prep(z, p, t):
    n = z.shape[0]
    tile = pl.BlockSpec((ROW_TILE, ZA_W), lambda i: (i, 0))
    prev, nxt = _halo_specs(ZA_W, 0, n, ROW_TILE)
    out = pl.BlockSpec((ROW_TILE, D_GROUP), lambda i: (i, 0))
    vec = _const((1, D_GROUP))
    return pl.pallas_call(
        functools.partial(_rwkv_prep_kernel, t),
        grid=(n // ROW_TILE,),
        in_specs=[tile, prev, nxt, _const((1, ZA_W)), vec, vec, vec,
                  _const((N_DIR, 1, D_GROUP)), _const((N_DIR, 1, D_GROUP)),
                  _const((N_DIR, 128, D_GROUP)), _const((N_DIR, 128, D_GROUP)), _const((128, D_GROUP))],
        out_specs=[out] * 11,
        out_shape=[jax.ShapeDtypeStruct((n, D_GROUP), F32)] * 11,
        compiler_params=_params("parallel"),
        name="rwkv_prep",
    )(z, z, z, p['mu'], p['k_k'], p['k_a'], p['r_k'], p['w0'], p['a0'], p['w_up'], p['a_up'], p['g_up'])


def _rwkv_chunk(r, kk, v, lw, kd, b, s_ref, d, y_ref, reverse):
    L = CHUNK
    incl = _tri(L, reverse, False)
    strict = _tri(L, reverse, True)
    cum = _mm_hi(incl.astype(F32), lw)
    last = cum[0:1, :] if reverse else cum[L - 1:L, :]
    e_in = jnp.exp(cum)
    e_ex = jnp.exp(cum - lw)
    e_neg = jnp.exp(-cum)
    e_end = jnp.exp(last - cum)
    e_all = jnp.exp(last)
    aq = kk * e_ex
    rq = r * e_in
    bk = b * e_neg
    kq = kd * e_neg
    k_end = kd * e_end
    b_end = b * e_end
    eye = (lax.broadcasted_iota(jnp.int32, (L, L), 0) == lax.broadcasted_iota(jnp.int32, (L, L), 1)).astype(F32)
    for h in range(N_HEADS):
        hs = slice(h * HEAD_DIM, (h + 1) * HEAD_DIM)
        s0 = s_ref[d, h]
        m = jnp.where(strict, _mm_nt(aq[:, hs], bk[:, hs], HI), 0.0)
        n = jnp.where(strict, _mm_nt(aq[:, hs], kq[:, hs], HI), 0.0)
        pb = jnp.where(incl, _mm_nt(rq[:, hs], bk[:, hs], HI), 0.0)
        pk = jnp.where(incl, _mm_nt(rq[:, hs], kq[:, hs], HI), 0.0)
        x = eye + m
        p = m
        for _ in range(int(math.log2(L)) - 1):
            p = _mm_hi(p, p)
            x = x + _mm_hi(x, p)
        vh = v[:, hs]
        u = _mm_hi(x, _mm_hi(n, vh) + _mm_nt(aq[:, hs], s0, HI))
        y = _mm_nt(rq[:, hs], s0, HI) + _mm_hi(pk, vh) + _mm_hi(pb, u)
        y_ref[:, hs] = y
        s_ref[d, h] = s0 * e_all[:, hs] + _mm_tn(vh, k_end[:, hs], HI) + _mm_tn(u, b_end[:, hs], HI)


def _rwkv_scan_kernel(has_init, *refs):
    if has_init:
        s0_ref, refs = refs[0], refs[1:]
    (rf, kkf, vf, lw0, kd0, b0, rb, kkb, vb, lw1, kd1, b1, y0_ref, y1_ref, sfin_ref, s_ref) = refs
    i = pl.program_id(1)

    @pl.when(i == 0)
    def _():
        if has_init:
            s_ref[...] = s0_ref[0]
        else:
            s_ref[...] = jnp.zeros_like(s_ref)

    _rwkv_chunk(rf[...], kkf[...], vf[...], lw0[...], kd0[...], b0[...], s_ref, 0, y0_ref, False)
    _rwkv_chunk(rb[...], kkb[...], vb[...], lw1[...], kd1[...], b1[...], s_ref, 1, y1_ref, True)

    @pl.when(i == pl.num_programs(1) - 1)
    def _():
        sfin_ref[0] = s_ref[...]


def rwkv_scan(r, kk, v, lw, kd, b, s0, bsz, t):
    nc = t // CHUNK
    fwd = pl.BlockSpec((CHUNK, D_GROUP), lambda bi, i: (bi * nc + i, 0))
    bwd = pl.BlockSpec((CHUNK, D_GROUP), lambda bi, i: (bi * nc + nc - 1 - i, 0))
    st = pl.BlockSpec((1, N_DIR, N_HEADS, HEAD_DIM, HEAD_DIM), lambda bi, i: (bi, 0, 0, 0, 0))
    has_init = s0 is not None
    args = ([s0] if has_init else []) + [r, kk, v, lw[0], kd[0], b[0], r, kk, v, lw[1], kd[1], b[1]]
    in_specs = ([st] if has_init else []) + [fwd] * 6 + [bwd] * 6
    tok = jax.ShapeDtypeStruct((bsz * t, D_GROUP), F32)
    return pl.pallas_call(
        functools.partial(_rwkv_scan_kernel, has_init),
        grid=(bsz, nc),
        in_specs=in_specs,
        out_specs=[fwd, bwd, st],
        out_shape=[tok, tok, jax.ShapeDtypeStruct((bsz, N_DIR, N_HEADS, HEAD_DIM, HEAD_DIM), F32)],
        scratch_shapes=[pltpu.VMEM((N_DIR, N_HEADS, HEAD_DIM, HEAD_DIM), F32)],
        compiler_params=_params("parallel", "arbitrary"),
        name="rwkv_scan",
    )(*args)


def _shift_rows(x, k, pos, n):
    rows = x.shape[0]
    rolled = pltpu.roll(x, k % rows, axis=0)
    ok = (pos >= k) if k > 0 else (pos < n + k)
    return jnp.where(ok, rolled, 0.0)


def _window_sums(x, pos, n):
    fw = x
    bw = _shift_rows(x, 1, pos, n)
    out = [fw + bw]
    for h in POOL_HALF[:-1]:
        fw = fw + _shift_rows(fw, -h, pos, n)
        bw = bw + _shift_rows(bw, h, pos, n)
        out.append(fw + bw)
    return out


def _by_group(vals, lane):
    out = vals[-1]
    for g in range(len(vals) - 2, -1, -1):
        out = jnp.where(lane < (g + 1) * HEAD_DIM, vals[g], out)
    return out


def _clipped_count(pos, h, n):
    return jnp.minimum(pos + h, n) - jnp.maximum(pos - h, 0)


def _pool_seq_kernel(x_ref, w_ref, sc_ref, o_ref):
    x = x_ref[...]
    t = x.shape[0]
    pos = lax.broadcasted_iota(jnp.int32, x.shape, 0)
    lane = lax.broadcasted_iota(jnp.int32, x.shape, 1)
    sums = _by_group(_window_sums(x, pos, t), lane)
    cnt = _by_group([_clipped_count(pos, h, t) for h in POOL_HALF], lane).astype(F32)
    o_ref[...] = _mm(sums / cnt - x, w_ref[...]) * sc_ref[...]


def _pool_grid_kernel(x_ref, w_ref, sc_ref, o_ref, cs_ref):
    gw = GRID_W
    rows = x_ref.shape[0] // gw
    shape = (gw, D_GROUP)
    col = lax.broadcasted_iota(jnp.int32, shape, 0)
    lane = lax.broadcasted_iota(jnp.int32, shape, 1)
    half = _by_group([jnp.full(shape, h, jnp.int32) for h in POOL_HALF], lane)
    cnt_c = _by_group([_clipped_count(col, h, gw) for h in POOL_HALF], lane)

    def col_stage(r, _):
        at = pl.ds(pl.multiple_of(r * gw, gw), gw)
        cs_ref[at, :] = _by_group(_window_sums(x_ref[at, :], col, gw), lane)
        return 0

    lax.fori_loop(0, rows, col_stage, 0)

    def row_stage(r, _):
        at = pl.ds(pl.multiple_of(r * gw, gw), gw)
        acc = jnp.zeros(shape, F32)
        for o in range(-MAX_HALF, MAX_HALF):
            rr = r + o
            src = pl.ds(pl.multiple_of(jnp.clip(rr, 0, rows - 1) * gw, gw), gw)
            in_window = (half > o) if o >= 0 else (half >= -o)
            ok = jnp.logical_and(in_window, jnp.logical_and(rr >= 0, rr < rows))
            acc = acc + jnp.where(ok, cs_ref[src, :], 0.0)
        cnt = (_clipped_count(r, half, rows) * cnt_c).astype(F32)
        o_ref[at, :] = _mm(acc / cnt - x_ref[at, :], w_ref[...]) * sc_ref[...]
        return 0

    lax.fori_loop(0, rows, row_stage, 0)


def pool_mix(z, w, scale, bsz, t, grid):
    n = bsz * t
    col = ZB_OFF // ZB_W
    scratch = [pltpu.VMEM((t, D_GROUP), F32)] if grid else []
    return pl.pallas_call(
        _pool_grid_kernel if grid else _pool_seq_kernel,
        grid=(bsz,),
        in_specs=[pl.BlockSpec((t, ZB_W), lambda i: (i, col)), _const((D_GROUP, D_GROUP)), _const((1, D_GROUP))],
        out_specs=pl.BlockSpec((t, D_GROUP), lambda i: (i, 0)),
        out_shape=jax.ShapeDtypeStruct((n, D_GROUP), F32),
        scratch_shapes=scratch,
        compiler_params=_params("parallel"),
        name="pool_grid" if grid else "pool_seq",
    )(z, w, scale)


CONV_TILE = 64


def _conv_kernel(z_ref, dw_ref, b_ref, lg_ref, lb_ref, pw_ref, o_ref, pad_ref):
    t = z_ref.shape[0]
    rt = CONV_TILE
    edge = jnp.zeros((CONV_PAD, D_GROUP), F32)
    pad_ref[0:CONV_PAD, :] = edge
    pad_ref[t + CONV_PAD:t + 2 * CONV_PAD, :] = edge

    def glu(i, _):
        base = pl.multiple_of(i * rt, rt)
        zz = z_ref[pl.ds(base, rt), :]
        pad_ref[pl.ds(base + CONV_PAD, rt), :] = zz[:, :D_GROUP] * _sigmoid(zz[:, D_GROUP:])
        return 0

    lax.fori_loop(0, t // rt, glu, 0)

    def tile(i, _):
        base = pl.multiple_of(i * rt, rt)
        acc = jnp.broadcast_to(b_ref[...], (rt, D_GROUP))
        rows = rt + 2 * CONV_PAD
        win = pad_ref[pl.ds(base, rows), :]
        first = CONV_PAD - CONV_W // 2
        for sub in range(SUB):
            shifted = pltpu.roll(win, (rows - sub) % rows, axis=0) if sub else win
            for j in range(CONV_W):
                off = first + j
                if off % SUB == sub:
                    acc = acc + dw_ref[j:j + 1, :] * shifted[off - sub:off - sub + rt, :]
        mu = jnp.mean(acc, axis=-1, keepdims=True)
        var = jnp.mean(jnp.square(acc - mu), axis=-1, keepdims=True)
        y = (acc - mu) * lax.rsqrt(var + CONV_LN_EPS) * lg_ref[...] + lb_ref[...]
        o_ref[pl.ds(base, rt), :] = _mm(y * _sigmoid(y), pw_ref[...])
        return 0

    lax.fori_loop(0, t // rt, tile, 0)


def conv_mix(z, p, bsz, t):
    n = bsz * t
    col = ZC_OFF // ZC_W
    vec = _const((1, D_GROUP))
    return pl.pallas_call(
        _conv_kernel,
        grid=(bsz,),
        in_specs=[pl.BlockSpec((t, ZC_W), lambda i: (i, col)), _const((CONV_W, D_GROUP)), vec, vec, vec,
                  _const((D_GROUP, D_GROUP))],
        out_specs=pl.BlockSpec((t, D_GROUP), lambda i: (i, 0)),
        out_shape=jax.ShapeDtypeStruct((n, D_GROUP), F32),
        scratch_shapes=[pltpu.VMEM((t + 2 * CONV_PAD, D_GROUP), F32)],
        compiler_params=_params("parallel"),
        name="conv_mix",
    )(z, p['dw'], p['b'], p['ln_g'], p['ln_b'], p['pw'])


def _mlstm_prep_kernel(t, z_ref, zp_ref, zn_ref, zg_ref, cw_ref, gb_ref, q_o, k_o, g_o):
    i = pl.program_id(0)
    z = z_ref[...]
    prev, nxt = _neighbours(z, zp_ref[...], zn_ref[...], i, ROW_TILE, t)
    qk = cw_ref[0:1, :] * prev + cw_ref[1:2, :] * z + cw_ref[2:3, :] * nxt
    qk = qk * _sigmoid(qk)
    q_o[...] = qk[:, :D_GROUP]
    k_o[...] = qk[:, D_GROUP:] * (1.0 / math.sqrt(HEAD_DIM))
    gb = zg_ref[...] + gb_ref[...]
    lane = lax.broadcasted_iota(jnp.int32, gb.shape, 1)
    g_o[...] = jnp.where(lane < N_DIR * N_HEADS, gb, -_softplus(-gb))


def mlstm_prep(z, p, t):
    n = z.shape[0]
    col = ZD_OFF // ZD_QK_W
    prev, nxt = _halo_specs(ZD_QK_W, col, n, ROW_TILE)
    out = pl.BlockSpec((ROW_TILE, D_GROUP), lambda i: (i, 0))
    gates = pl.BlockSpec((ROW_TILE, ZG_W), lambda i: (i, ZG_OFF // ZG_W))
    return pl.pallas_call(
        functools.partial(_mlstm_prep_kernel, t),
        grid=(n // ROW_TILE,),
        in_specs=[pl.BlockSpec((ROW_TILE, ZD_QK_W), lambda i: (i, col)), prev, nxt, gates,
                  _const((3, ZD_QK_W)), _const((1, ZG_W))],
        out_specs=[out, out, pl.BlockSpec((ROW_TILE, ZG_W), lambda i: (i, 0))],
        out_shape=[jax.ShapeDtypeStruct((n, D_GROUP), F32)] * 2 + [jax.ShapeDtypeStruct((n, ZG_W), F32)],
        compiler_params=_params("parallel"),
        name="mlstm_prep",
    )(z, z, z, z, p['qk_conv'], p['gate_bias'])


def _mlstm_chunk(q, k, v, g, c_ref, n_ref, m_ref, d, h_ref, reverse):
    L = CHUNK
    nh = N_DIR * N_HEADS
    incl = _tri(L, reverse, False)
    bc_all = _mm_hi(incl.astype(F32), g)
    src = g - pltpu.roll(bc_all, ZG_W - nh, axis=1)
    ones = jnp.ones((L, ZG_W), F32)
    lane = lax.broadcasted_iota(jnp.int32, (L, ZG_W), 1)
    last = 0 if reverse else L - 1
    for h in range(N_HEADS):
        j = d * N_HEADS + h
        hs = slice(h * HEAD_DIM, (h + 1) * HEAD_DIM)
        qh, kh, vh = q[:, hs], k[:, hs], v[:, hs]
        c = c_ref[d, h]
        n = n_ref[j:j + 1, :]
        m = m_ref[j:j + 1, 0:1]
        li = g[:, j:j + 1]
        bcum = bc_all[:, nh + j:nh + j + 1]
        per_src = _mm_nt(ones, jnp.where(lane == j, src, 0.0), HI)
        log_w = jnp.where(incl, bcum + per_src, -jnp.inf)
        m_t = jnp.maximum(bcum + m, jnp.max(log_w, axis=1, keepdims=True))
        w = jnp.exp(log_w - m_t)
        inter = jnp.exp(bcum + m - m_t)
        s = _mm_nt(qh, kh) * w
        num = _mm(s, vh) + inter * _mm_nt(qh, c)
        den = jnp.sum(s, axis=1, keepdims=True) + inter * jnp.sum(qh * n, axis=1, keepdims=True)
        h_ref[:, hs] = num / jnp.maximum(jnp.abs(den), jnp.exp(-m_t))
        m_new = m_t[last:last + 1, :]
        b_last = bcum[last:last + 1, :]
        w_end = jnp.exp(b_last - bcum + li - m_new)
        carry = jnp.exp(b_last + m - m_new)
        c_ref[d, h] = carry * c + _mm_tn(w_end * vh, kh)
        n_ref[j:j + 1, :] = carry * n + jnp.sum(w_end * kh, axis=0, keepdims=True)
        m_ref[j:j + 1, :] = jnp.broadcast_to(m_new, (1, ZG_W))


def _mlstm_scan_kernel(has_init, *refs):
    if has_init:
        c0_ref, n0_ref, m0_ref = refs[:3]
        refs = refs[3:]
    (qf, kf, vf, gf, qb, kb, vb, gb, h0_ref, h1_ref, cfin_ref, nfin_ref, mfin_ref, c_ref, n_ref, m_ref) = refs
    i = pl.program_id(1)

    @pl.when(i == 0)
    def _():
        if has_init:
            c_ref[...] = c0_ref[0]
            n_ref[...] = n0_ref[0]
            m_ref[...] = m0_ref[0]
        else:
            c_ref[...] = jnp.zeros_like(c_ref)
            n_ref[...] = jnp.zeros_like(n_ref)
            m_ref[...] = jnp.zeros_like(m_ref)

    _mlstm_chunk(qf[...], kf[...], vf[...], gf[...], c_ref, n_ref, m_ref, 0, h0_ref, False)
    _mlstm_chunk(qb[...], kb[...], vb[...], gb[...], c_ref, n_ref, m_ref, 1, h1_ref, True)

    @pl.when(i == pl.num_programs(1) - 1)
    def _():
        cfin_ref[0] = c_ref[...]
        nfin_ref[0] = n_ref[...]
        mfin_ref[0] = m_ref[...]


def mlstm_scan(q, k, z, g, init, bsz, t):
    nc = t // CHUNK
    nh = N_DIR * N_HEADS
    vcol = (ZD_OFF + 2 * D_GROUP) // D_GROUP

    def specs(row):
        return [pl.BlockSpec((CHUNK, D_GROUP), lambda bi, i: (row(bi, i), 0)),
                pl.BlockSpec((CHUNK, D_GROUP), lambda bi, i: (row(bi, i), 0)),
                pl.BlockSpec((CHUNK, D_GROUP), lambda bi, i: (row(bi, i), vcol)),
                pl.BlockSpec((CHUNK, ZG_W), lambda bi, i: (row(bi, i), 0))]

    fwd = specs(lambda bi, i: bi * nc + i)
    bwd = specs(lambda bi, i: bi * nc + nc - 1 - i)
    cst = pl.BlockSpec((1, N_DIR, N_HEADS, HEAD_DIM, HEAD_DIM), lambda bi, i: (bi, 0, 0, 0, 0))
    nst = pl.BlockSpec((1, nh, HEAD_DIM), lambda bi, i: (bi, 0, 0))
    mst = pl.BlockSpec((1, nh, ZG_W), lambda bi, i: (bi, 0, 0))
    has_init = init is not None
    args = (list(init) if has_init else []) + [q, k, z, g] * 2
    tok = jax.ShapeDtypeStruct((bsz * t, D_GROUP), F32)
    return pl.pallas_call(
        functools.partial(_mlstm_scan_kernel, has_init),
        grid=(bsz, nc),
        in_specs=([cst, nst, mst] if has_init else []) + fwd + bwd,
        out_specs=[fwd[0], bwd[0], cst, nst, mst],
        out_shape=[tok, tok,
                   jax.ShapeDtypeStruct((bsz, N_DIR, N_HEADS, HEAD_DIM, HEAD_DIM), F32),
                   jax.ShapeDtypeStruct((bsz, nh, HEAD_DIM), F32),
                   jax.ShapeDtypeStruct((bsz, nh, ZG_W), F32)],
        scratch_shapes=[pltpu.VMEM((N_DIR, N_HEADS, HEAD_DIM, HEAD_DIM), F32),
                        pltpu.VMEM((nh, HEAD_DIM), F32),
                        pltpu.VMEM((nh, ZG_W), F32)],
        compiler_params=_params("parallel", "arbitrary"),
        name="mlstm_scan",
    )(*args)


def _mix_out_kernel(x_ref, mod_ref, y0_ref, y1_ref, bon_ref, g_ref, lng_ref, lnb_ref, yb_ref, yc_ref,
                    h0_ref, h1_ref, o_ref, hng_ref, w_ref, out_ref):
    ones = _head_ones()
    inv = 1.0 / HEAD_DIM
    y = y0_ref[...] + y1_ref[...] + bon_ref[...]
    mu = _mm_hi(y, ones) * inv
    yc = y - mu
    var = _mm_hi(yc * yc, ones) * inv
    ya = (yc * lax.rsqrt(var + RWKV_LN_EPS) * lng_ref[...] + lnb_ref[...]) * g_ref[...]
    hs = (h0_ref[...] + h1_ref[...]) * _sigmoid(o_ref[...])
    yd = hs * lax.rsqrt(_mm_hi(hs * hs, ones) * inv + EPS) * hng_ref[...]
    g = D_GROUP
    mixed = (_mm(ya, w_ref[0:g, :]) + _mm(yb_ref[...], w_ref[g:2 * g, :])
             + _mm(yc_ref[...], w_ref[2 * g:3 * g, :]) + _mm(yd, w_ref[3 * g:4 * g, :]))
    out_ref[...] = x_ref[...] + mod_ref[0, 2:3, :] * mixed


def mix_out(x, mod, ra, yb, yc, md, z, p, t):
    n = x.shape[0]
    tok = pl.BlockSpec((ROW_TILE, D_GROUP), lambda i: (i, 0))
    ocol = (ZD_OFF + 3 * D_GROUP) // D_GROUP
    vec = _const((1, D_GROUP))
    row = pl.BlockSpec((ROW_TILE, D_MODEL), lambda i: (i, 0))
    return pl.pallas_call(
        _mix_out_kernel,
        grid=(n // ROW_TILE,),
        in_specs=[row, _mod_spec(mod.shape[0] > 1, t), tok, tok, tok, tok, vec, vec, tok, tok, tok, tok,
                  pl.BlockSpec((ROW_TILE, D_GROUP), lambda i: (i, ocol)), vec, _const((D_MODEL, D_MODEL))],
        out_specs=row,
        out_shape=jax.ShapeDtypeStruct((n, D_MODEL), F32),
        compiler_params=_params("parallel"),
        name="mix_out",
    )(x, mod, ra['y0'], ra['y1'], ra['bon'], ra['g'], p['rwkv_ln_g'], p['rwkv_ln_b'], yb, yc,
      md['h0'], md['h1'], z, p['hn_g'], p['w_out'])


def _mlp_kernel(final, x_ref, mod_ref, g_ref, w1_ref, b1_ref, w2_ref, b2_ref, fg_ref, out_ref):
    x = x_ref[...]
    h = _rms(x, g_ref[...]) * (1.0 + mod_ref[0, 4:5, :]) + mod_ref[0, 3:4, :]
    a = jnp.maximum(jnp.dot(h.astype(BF16), w1_ref[...], preferred_element_type=F32) + b1_ref[...], 0.0)
    f = jnp.dot((a * a).astype(BF16), w2_ref[...], preferred_element_type=F32) + b2_ref[...]
    x = x + mod_ref[0, 5:6, :] * f
    out_ref[...] = _rms(x, fg_ref[...]) if final else x


def mlp(x, mod, p, final_g, t, final):
    n = x.shape[0]
    row = pl.BlockSpec((ROW_TILE, D_MODEL), lambda i: (i, 0))
    return pl.pallas_call(
        functools.partial(_mlp_kernel, final),
        grid=(n // ROW_TILE,),
        in_specs=[row, _mod_spec(mod.shape[0] > 1, t), _const((1, D_MODEL)), _const((D_MODEL, D_FF)),
                  _const((1, D_FF)), _const((D_FF, D_MODEL)), _const((1, D_MODEL)), _const((1, D_MODEL))],
        out_specs=row,
        out_shape=jax.ShapeDtypeStruct((n, D_MODEL), F32),
        compiler_params=_params("parallel"),
        name="mlp",
    )(x, mod, p['norm2_g'], p['w1'], p['b1'], p['w2'], p['b2'], final_g)


def _trunk_layer(x, mod, p, states, bsz, t, grid, final_g, final):
    z = in_proj(x, mod, p['norm1_g'], p['w_in'], t)
    r, kk, v, g, bon, lw0, lw1, kd0, kd1, b0, b1 = rwkv_prep(z, p['rwkv'], t)
    y0, y1, s_rwkv = rwkv_scan(r, kk, v, (lw0, lw1), (kd0, kd1), (b0, b1),
                               None if states is None else states[0], bsz, t)
    yb = pool_mix(z, p['pool_w'], p['pool_scale'], bsz, t, grid)
    yc = conv_mix(z, p['conv'], bsz, t)
    q, k, gates = mlstm_prep(z, p['mlstm'], t)
    h0, h1, s_c, s_n, s_m = mlstm_scan(q, k, z, gates, None if states is None else states[1:], bsz, t)
    x = mix_out(x, mod, dict(y0=y0, y1=y1, bon=bon, g=g), yb, yc, dict(h0=h0, h1=h1), z, p, t)
    x = mlp(x, mod, p, final_g, t, final)
    return x, (s_rwkv, s_c, s_n, s_m)


def _layer_params(l, norm1_g, norm2_g, w_in, w_out, rwkv_mu, rwkv_w0, rwkv_w_up, rwkv_a0, rwkv_a_up, rwkv_g_up,
                  rwkv_k_k, rwkv_k_a, rwkv_r_k, rwkv_ln_g, rwkv_ln_b, pool_w, pool_scale, conv_dw, conv_b,
                  conv_ln_g, conv_ln_b, conv_pw, mlstm_qk_conv, mlstm_i_bias, mlstm_f_bias, mlstm_hn_g,
                  mlp_w1, mlp_b1, mlp_w2, mlp_b2):
    row = lambda a: a[l].reshape(1, -1)
    w = w_in[l]
    a_w, b_w, c_w = 1024, 256, 512
    wa, wb, wc, wd = w[:, :a_w], w[:, a_w:a_w + b_w], w[:, a_w + b_w:a_w + b_w + c_w], w[:, a_w + b_w + c_w:]
    nh = N_DIR * N_HEADS
    qkv, gates, o = wd[:, :3 * D_GROUP], wd[:, 3 * D_GROUP:3 * D_GROUP + 2 * nh], wd[:, 3 * D_GROUP + 2 * nh:]
    w_cat = jnp.concatenate([wa, qkv, o, wc, wb, gates, jnp.zeros((D_MODEL, ZG_W - 2 * nh), F32)], axis=1)
    rank = rwkv_w_up.shape[2]
    zeros = jnp.zeros((N_DIR, rank, D_GROUP), F32)
    pool_bd = jax.scipy.linalg.block_diag(*[pool_w[l, g] for g in range(len(POOL_HALF))])
    gate_bias = jnp.concatenate([mlstm_i_bias[l], mlstm_f_bias[l], jnp.zeros((ZG_W - 2 * nh,), F32)]).reshape(1, ZG_W)
    return dict(
        norm1_g=row(norm1_g), norm2_g=row(norm2_g), w_in=w_cat.astype(BF16), w_out=w_out[l].astype(BF16),
        rwkv=dict(mu=row(rwkv_mu), k_k=row(rwkv_k_k), k_a=row(rwkv_k_a), r_k=row(rwkv_r_k),
                  w0=rwkv_w0[l].reshape(N_DIR, 1, D_GROUP), a0=rwkv_a0[l].reshape(N_DIR, 1, D_GROUP),
                  w_up=jnp.concatenate([rwkv_w_up[l], zeros], axis=1).astype(BF16),
                  a_up=jnp.concatenate([zeros, rwkv_a_up[l]], axis=1).astype(BF16),
                  g_up=rwkv_g_up[l].astype(BF16)),
        rwkv_ln_g=row(rwkv_ln_g), rwkv_ln_b=row(rwkv_ln_b),
        pool_w=pool_bd.astype(BF16), pool_scale=row(pool_scale),
        conv=dict(dw=conv_dw[l], b=row(conv_b), ln_g=row(conv_ln_g), ln_b=row(conv_ln_b), pw=conv_pw[l].astype(BF16)),
        mlstm=dict(qk_conv=mlstm_qk_conv[l], gate_bias=gate_bias),
        hn_g=row(mlstm_hn_g),
        w1=mlp_w1[l].astype(BF16), b1=row(mlp_b1), w2=mlp_w2[l].astype(BF16), b2=row(mlp_b2),
    )


def kernel(x_prompt, x_sample, c, state_rwkv, state_mlstm_C, state_mlstm_n, state_mlstm_m, c_ctx, w_mod, b_mod, norm1_g, norm2_g, w_in, w_out, rwkv_mu, rwkv_w0, rwkv_w_up, rwkv_a0, rwkv_a_up, rwkv_g_up, rwkv_k_k, rwkv_k_a, rwkv_r_k, rwkv_ln_g, rwkv_ln_b, pool_w, pool_scale, conv_dw, conv_b, conv_ln_g, conv_ln_b, conv_pw, mlstm_qk_conv, mlstm_i_bias, mlstm_f_bias, mlstm_hn_g, mlp_w1, mlp_b1, mlp_w2, mlp_b2, final_g):
    bp, tp, d = x_prompt.shape
    bs, ts, _ = x_sample.shape
    nh = N_DIR * N_HEADS
    cvec = jnp.concatenate([c_ctx[None, :], c, jnp.zeros((SUB - 1 - bs, d), F32)], axis=0)
    mods = modulation(cvec, w_mod, b_mod).reshape(DEPTH, SUB, 6, d)
    xp = x_prompt.reshape(bp * tp, d)
    xs = x_sample.reshape(bs * ts, d)
    fg = final_g.reshape(1, d)
    new = [[], [], [], []]
    for l in range(DEPTH):
        p = _layer_params(l, norm1_g, norm2_g, w_in, w_out, rwkv_mu, rwkv_w0, rwkv_w_up, rwkv_a0, rwkv_a_up,
                          rwkv_g_up, rwkv_k_k, rwkv_k_a, rwkv_r_k, rwkv_ln_g, rwkv_ln_b, pool_w, pool_scale,
                          conv_dw, conv_b, conv_ln_g, conv_ln_b, conv_pw, mlstm_qk_conv, mlstm_i_bias,
                          mlstm_f_bias, mlstm_hn_g, mlp_w1, mlp_b1, mlp_w2, mlp_b2)
        final = l == DEPTH - 1
        xp, st = _trunk_layer(xp, mods[l, 0:1], p, None, bp, tp, False, fg, final)
        for acc, s in zip(new, st):
            acc.append(s)
        cached = (state_rwkv[:, l], state_mlstm_C[:, l], state_mlstm_n[:, l].reshape(bs, nh, HEAD_DIM),
                  jnp.broadcast_to(state_mlstm_m[:, l].reshape(bs, nh, 1), (bs, nh, ZG_W)))
        xs, _ = _trunk_layer(xs, mods[l, 1:1 + bs], p, cached, bs, ts, True, fg, final)
    return (xp.reshape(bp, tp, d), xs.reshape(bs, ts, d),
            jnp.stack(new[0], axis=1), jnp.stack(new[1], axis=1),
            jnp.stack(new[2], axis=1).reshape(bp, DEPTH, N_DIR, N_HEADS, HEAD_DIM),
            jnp.stack(new[3], axis=1)[..., 0].reshape(bp, DEPTH, N_DIR, N_HEADS))
```

```python
import functools
import math

import jax
import jax.numpy as jnp
from jax import lax
from jax.experimental import pallas as pl
from jax.experimental.pallas import tpu as pltpu

F32 = jnp.float32
BF16 = jnp.bfloat16

D_MODEL = 1024
D_GROUP = 256
HEAD_DIM = 64
N_HEADS = 4
N_DIR = 2
DEPTH = 4
GRID_W = 64
POOL_HALF = (1, 2, 4, 8)
MAX_HALF = POOL_HALF[-1]
CONV_W = 31
CONV_PAD = 16
D_FF = 4 * D_MODEL
EPS = 1e-6
RWKV_LN_EPS = 64e-5
CONV_LN_EPS = 1e-5
CHUNK = 64
ROW_TILE = 256
SUB = 8
VMEM_LIMIT = 56 * 1024 * 1024
HI = lax.Precision.HIGHEST

Z_W = 2944
ZA_W, ZD_QK_W, ZC_W, ZB_W, ZG_W = 1024, 512, 512, 256, 128
ZD_OFF, ZC_OFF, ZB_OFF, ZG_OFF = 1024, 2048, 2560, 2816


def _mm(a, b):
    return jnp.dot(a.astype(BF16), b.astype(BF16), preferred_element_type=F32)


def _mm_nt(a, b, precision=None):
    if precision is None:
        a, b = a.astype(BF16), b.astype(BF16)
    return lax.dot_general(a, b, (((1,), (1,)), ((), ())), precision=precision, preferred_element_type=F32)


def _mm_hi(a, b):
    return jnp.dot(a, b, precision=HI, preferred_element_type=F32)


def _tri(n, reverse, strict):
    t = lax.broadcasted_iota(jnp.int32, (n, n), 0)
    s = lax.broadcasted_iota(jnp.int32, (n, n), 1)
    if reverse:
        return (s > t) if strict else (s >= t)
    return (s < t) if strict else (s <= t)


def _head_ones():
    a = lax.broadcasted_iota(jnp.int32, (D_GROUP, D_GROUP), 0) // HEAD_DIM
    b = lax.broadcasted_iota(jnp.int32, (D_GROUP, D_GROUP), 1) // HEAD_DIM
    return (a == b).astype(F32)


def _sigmoid(x):
    return jax.nn.sigmoid(x)


def _softplus(x):
    return jnp.maximum(x, 0.0) + jnp.log1p(jnp.exp(-jnp.abs(x)))


def _rms(x, g):
    return x * lax.rsqrt(jnp.mean(x * x, axis=-1, keepdims=True) + EPS) * g


def _params(*sem):
    return pltpu.CompilerParams(dimension_semantics=sem, vmem_limit_bytes=VMEM_LIMIT)


def _const(shape):
    return pl.BlockSpec(shape, lambda *_: (0,) * len(shape))


def _neighbours(z, prev_blk, next_blk, i, tt, t):
    row = lax.broadcasted_iota(jnp.int32, z.shape, 0)
    at_start = (i * tt) % t == 0
    at_end = ((i + 1) * tt) % t == 0
    first = jnp.where(at_start, 0.0, prev_blk[SUB - 1:SUB, :])
    last = jnp.where(at_end, 0.0, next_blk[0:1, :])
    prev = jnp.where(row == 0, first, pltpu.roll(z, 1, axis=0))
    nxt = jnp.where(row == tt - 1, last, pltpu.roll(z, tt - 1, axis=0))
    return prev, nxt


def _halo_specs(width, col, n_rows, tt):
    per = tt // SUB
    last = n_rows // SUB - 1
    prev = pl.BlockSpec((SUB, width), lambda i: (jnp.maximum(i * per - 1, 0), col))
    nxt = pl.BlockSpec((SUB, width), lambda i: (jnp.minimum((i + 1) * per, last), col))
    return prev, nxt


def _mod_kernel(c_ref, w_ref, b_ref, o_ref):
    c = c_ref[...]
    o_ref[0] = _mm(c * _sigmoid(c), w_ref[0]) + b_ref[0]


def modulation(cvec, w_mod, b_mod):
    tn = 1536
    n = w_mod.shape[-1]
    return pl.pallas_call(
        _mod_kernel,
        grid=(DEPTH, n // tn),
        in_specs=[pl.BlockSpec((SUB, D_MODEL), lambda l, j: (0, 0)),
                  pl.BlockSpec((1, D_MODEL, tn), lambda l, j: (l, 0, j)),
                  pl.BlockSpec((1, 1, tn), lambda l, j: (l, 0, j))],
        out_specs=pl.BlockSpec((1, SUB, tn), lambda l, j: (l, 0, j)),
        out_shape=jax.ShapeDtypeStruct((DEPTH, SUB, n), F32),
        compiler_params=_params("parallel", "parallel"),
        name="modulation",
    )(cvec, w_mod, b_mod.reshape(DEPTH, 1, n))


def _mod_spec(per_batch, rows_per_seq):
    if per_batch:
        return pl.BlockSpec((1, 6, D_MODEL), lambda i: (i * ROW_TILE // rows_per_seq, 0, 0))
    return pl.BlockSpec((1, 6, D_MODEL), lambda i: (0, 0, 0))


def _in_proj_kernel(x_ref, mod_ref, g_ref, w_ref, z_ref):
    h = _rms(x_ref[...], g_ref[...]) * (1.0 + mod_ref[0, 1:2, :]) + mod_ref[0, 0:1, :]
    z_ref[...] = jnp.dot(h.astype(BF16), w_ref[...], preferred_element_type=F32)


def in_proj(x, mod, g, w, t):
    n = x.shape[0]
    return pl.pallas_call(
        _in_proj_kernel,
        grid=(n // ROW_TILE,),
        in_specs=[pl.BlockSpec((ROW_TILE, D_MODEL), lambda i: (i, 0)),
                  _mod_spec(mod.shape[0] > 1, t),
                  _const((1, D_MODEL)),
                  _const((D_MODEL, Z_W))],
        out_specs=pl.BlockSpec((ROW_TILE, Z_W), lambda i: (i, 0)),
        out_shape=jax.ShapeDtypeStruct((n, Z_W), F32),
        compiler_params=_params("parallel"),
        name="in_proj",
    )(x, mod, g, w)


def _rwkv_prep_kernel(t, z_ref, zp_ref, zn_ref, mu_ref, kk_ref, ka_ref, rk_ref, w0_ref, a0_ref,
                      wup_ref, aup_ref, gup_ref,
                      r_o, kk_o, v_o, g_o, bon_o, lw0_o, lw1_o, kd0_o, kd1_o, b0_o, b1_o):
    i = pl.program_id(0)
    z = z_ref[...]
    prev, nxt = _neighbours(z, zp_ref[...], zn_ref[...], i, ROW_TILE, t)
    zm = z + (0.5 * (prev + nxt) - z) * mu_ref[...]
    r, k, v = zm[:, 0:256], zm[:, 256:512], zm[:, 512:768]
    lora = zm[:, 768:896]
    gd = zm[:, 896:1024]
    ones = _head_ones()
    kk = k * kk_ref[...]
    kk = kk * lax.rsqrt(_mm_hi(kk * kk, ones) + 1e-12)
    wt = jnp.tanh(lora)
    r_o[...] = r
    kk_o[...] = kk
    v_o[...] = v
    g_o[...] = _mm(_sigmoid(gd), gup_ref[...])
    bon = jnp.zeros_like(r)
    for d, (lw_o, kd_o, b_o) in enumerate(((lw0_o, kd0_o, b0_o), (lw1_o, kd1_o, b1_o))):
        w_log = -_softplus(-(w0_ref[d] + _mm(wt, wup_ref[d]))) - 0.5
        lw_o[...] = -jnp.exp(w_log)
        a = _sigmoid(a0_ref[d] + _mm(lora, aup_ref[d]))
        kd = k * (1.0 + (a - 1.0) * ka_ref[...])
        kd_o[...] = kd
        b_o[...] = -a * kk
        bon = bon + _mm_hi(r * kd * rk_ref[...], ones) * v
    bon_o[...] = bon


def rwkv_prep(z, p, t):
    n = z.shape[0]
    tile = pl.BlockSpec((ROW_TILE, ZA_W), lambda i: (i, 0))
    prev, nxt = _halo_specs(ZA_W, 0, n, ROW_TILE)
    out = pl.BlockSpec((ROW_TILE, D_GROUP), lambda i: (i, 0))
    vec = _const((1, D_GROUP))
    return pl.pallas_call(
        functools.partial(_rwkv_prep_kernel, t),
        grid=(n // ROW_TILE,),
        in_specs=[tile, prev, nxt, _const((1, ZA_W)), vec, vec, vec,
                  _const((N_DIR, 1, D_GROUP)), _const((N_DIR, 1, D_GROUP)),
                  _const((N_DIR, 128, D_GROUP)), _const((N_DIR, 128, D_GROUP)), _const((128, D_GROUP))],
        out_specs=[out] * 11,
        out_shape=[jax.ShapeDtypeStruct((n, D_GROUP), F32)] * 11,
        compiler_params=_params("parallel"),
        name="rwkv_prep",
    )(z, z, z, p['mu'], p['k_k'], p['k_a'], p['r_k'], p['w0'], p['a0'], p['w_up'], p['a_up'], p['g_up'])


SCAN_CB = 4


def _bf(x):
    return x.astype(BF16)


def _dot(a, b):
    return jnp.dot(a, b, preferred_element_type=F32)


def _dot_nt(a, b):
    return lax.dot_general(a, b, (((1,), (1,)), ((), ())), preferred_element_type=F32)


def _dot_tn(a, b):
    return lax.dot_general(a, b, (((0,), (0,)), ((), ())), preferred_element_type=F32)


def _chunk_rows(c, cb, reverse):
    return slice((cb - 1 - c) * CHUNK, (cb - c) * CHUNK) if reverse else slice(c * CHUNK, (c + 1) * CHUNK)


def _rwkv_scan_kernel(has_init, cb, *refs):
    if has_init:
        s0_ref, refs = refs[0], refs[1:]
    ins, (y0_ref, y1_ref, sfin_ref, s_ref) = refs[:12], refs[12:]
    L = CHUNK
    i = pl.program_id(1)

    @pl.when(i == 0)
    def _():
        if has_init:
            s_ref[...] = s0_ref[0]
        else:
            s_ref[...] = jnp.zeros_like(s_ref)

    eye = (lax.broadcasted_iota(jnp.int32, (L, L), 0) == lax.broadcasted_iota(jnp.int32, (L, L), 1)).astype(F32)
    heads = [slice(h * HEAD_DIM, (h + 1) * HEAD_DIM) for h in range(N_HEADS)]

    blocks = []
    for d in range(N_DIR):
        reverse = d == 1
        incl = _tri(L, reverse, False)
        for c in range(cb):
            rows = _chunk_rows(c, cb, reverse)
            blocks.append(dict(d=d, c=c, rows=rows, reverse=reverse, incl=incl, strict=_tri(L, reverse, True),
                               lw=ins[6 * d + 3][rows, :]))
    for u in blocks:
        u['cum'] = _mm_hi(u['incl'].astype(F32), u['lw'])
    inst = []
    for u in blocks:
        d, rows, cum, lw = u['d'], u['rows'], u['cum'], u['lw']
        r_ref, kk_ref, v_ref, _, kd_ref, b_ref = ins[6 * d:6 * d + 6]
        r, kk, v, kd, b = (ref[rows, :] for ref in (r_ref, kk_ref, v_ref, kd_ref, b_ref))
        last = cum[0:1, :] if u['reverse'] else cum[L - 1:L, :]
        e_neg = jnp.exp(-cum)
        e_end = jnp.exp(last - cum)
        e_all = jnp.exp(last)
        aq = _bf(kk * jnp.exp(cum - lw))
        rq = r * jnp.exp(cum)
        bk = _bf(b * e_neg)
        kq = _bf(kd * e_neg)
        k_end = _bf(kd * e_end)
        b_end = _bf(b * e_end)
        rq_b, v_b = _bf(rq), _bf(v)
        for h, hs in enumerate(heads):
            inst.append(dict(d=d, c=u['c'], h=h, incl=u['incl'], strict=u['strict'], aq=aq[:, hs],
                             rq=rq[:, hs], rq_b=rq_b[:, hs], bk=bk[:, hs], kq=kq[:, hs], k_end=k_end[:, hs],
                             b_end=b_end[:, hs], v=v_b[:, hs], e_all=e_all[:, hs]))

    for t in inst:
        t['m'] = jnp.where(t['strict'], _dot_nt(t['aq'], t['bk']), 0.0)
        t['n'] = _bf(jnp.where(t['strict'], _dot_nt(t['aq'], t['kq']), 0.0))
        t['pb'] = _bf(jnp.where(t['incl'], _dot_nt(t['rq_b'], t['bk']), 0.0))
        t['pk'] = _bf(jnp.where(t['incl'], _dot_nt(t['rq_b'], t['kq']), 0.0))
    row = lax.broadcasted_iota(jnp.int32, (L, L), 0)
    col = lax.broadcasted_iota(jnp.int32, (L, L), 1)
    sizes = [2 ** e for e in range(int(math.log2(L)))]
    corner = {s: jnp.logical_and(row // (2 * s) == col // (2 * s), row // s != col // s) for s in sizes}
    for t in inst:
        t['y'] = eye + jnp.where(corner[1], t['m'], 0.0)
    for s in sizes[1:]:
        for t in inst:
            t['my'] = _bf(_dot(_bf(jnp.where(corner[s], t['m'], 0.0)), _bf(t['y'])))
        for t in inst:
            t['y'] = t['y'] + _dot(_bf(t['y']), t['my'])
    for t in inst:
        t['y'] = _bf(t['y'])
        t['nv'] = _bf(_dot(t['n'], t['v']))
    for t in inst:
        t['w'] = _bf(_dot(t['y'], t['nv']))
        t['q'] = _bf(_dot(t['y'], t['aq']))
    for t in inst:
        t['y_loc'] = _dot(t['pk'], t['v']) + _dot(t['pb'], t['w'])
        t['r2'] = _bf(t['rq'] + _dot(t['pb'], t['q']))
        t['tq'] = _bf(_dot_tn(t['q'], t['b_end']))
        t['g'] = _dot_tn(t['v'], t['k_end']) + _dot_tn(t['w'], t['b_end'])

    state = [[s_ref[d, h] for h in range(N_HEADS)] for d in range(N_DIR)]
    outs = {}
    for c in range(cb):
        now = [t for t in inst if t['c'] == c]
        for t in now:
            s0 = state[t['d']][t['h']]
            s0_b = _bf(s0)
            outs[(t['d'], c, t['h'])] = t['y_loc'] + _dot_nt(t['r2'], s0_b)
            t['s1'] = s0 * t['e_all'] + _dot(s0_b, t['tq']) + t['g']
        for t in now:
            state[t['d']][t['h']] = t['s1']
    for d, y_ref in enumerate((y0_ref, y1_ref)):
        for c in range(cb):
            y_ref[_chunk_rows(c, cb, d == 1), :] = jnp.concatenate([outs[(d, c, h)] for h in range(N_HEADS)], axis=1)
    for d in range(N_DIR):
        for h in range(N_HEADS):
            s_ref[d, h] = state[d][h]

    @pl.when(i == pl.num_programs(1) - 1)
    def _():
        sfin_ref[0] = s_ref[...]


def rwkv_scan(r, kk, v, lw, kd, b, s0, bsz, t):
    cb = SCAN_CB
    blk = cb * CHUNK
    assert t % blk == 0
    nc = t // blk
    fwd = pl.BlockSpec((blk, D_GROUP), lambda bi, i: (bi * nc + i, 0))
    bwd = pl.BlockSpec((blk, D_GROUP), lambda bi, i: (bi * nc + nc - 1 - i, 0))
    st = pl.BlockSpec((1, N_DIR, N_HEADS, HEAD_DIM, HEAD_DIM), lambda bi, i: (bi, 0, 0, 0, 0))
    has_init = s0 is not None
    args = ([s0] if has_init else []) + [r, kk, v, lw[0], kd[0], b[0], r, kk, v, lw[1], kd[1], b[1]]
    in_specs = ([st] if has_init else []) + [fwd] * 6 + [bwd] * 6
    tok = jax.ShapeDtypeStruct((bsz * t, D_GROUP), F32)
    return pl.pallas_call(
        functools.partial(_rwkv_scan_kernel, has_init, cb),
        grid=(bsz, nc),
        in_specs=in_specs,
        out_specs=[fwd, bwd, st],
        out_shape=[tok, tok, jax.ShapeDtypeStruct((bsz, N_DIR, N_HEADS, HEAD_DIM, HEAD_DIM), F32)],
        scratch_shapes=[pltpu.VMEM((N_DIR, N_HEADS, HEAD_DIM, HEAD_DIM), F32)],
        compiler_params=_params("parallel", "arbitrary"),
        name="rwkv_scan",
    )(*args)


def _shift_rows(x, k, pos, n):
    rows = x.shape[0]
    rolled = pltpu.roll(x, k % rows, axis=0)
    ok = (pos >= k) if k > 0 else (pos < n + k)
    return jnp.where(ok, rolled, 0.0)


def _window_sums(x, pos, n):
    fw = x
    bw = _shift_rows(x, 1, pos, n)
    out = [fw + bw]
    for h in POOL_HALF[:-1]:
        fw = fw + _shift_rows(fw, -h, pos, n)
        bw = bw + _shift_rows(bw, h, pos, n)
        out.append(fw + bw)
    return out


def _by_group(vals, lane):
    out = vals[-1]
    for g in range(len(vals) - 2, -1, -1):
        out = jnp.where(lane < (g + 1) * HEAD_DIM, vals[g], out)
    return out


def _clipped_count(pos, h, n):
    return jnp.minimum(pos + h, n) - jnp.maximum(pos - h, 0)


def _pool_seq_kernel(x_ref, w_ref, sc_ref, o_ref):
    x = x_ref[...]
    t = x.shape[0]
    pos = lax.broadcasted_iota(jnp.int32, x.shape, 0)
    lane = lax.broadcasted_iota(jnp.int32, x.shape, 1)
    sums = _by_group(_window_sums(x, pos, t), lane)
    cnt = _by_group([_clipped_count(pos, h, t) for h in POOL_HALF], lane).astype(F32)
    o_ref[...] = _mm(sums / cnt - x, w_ref[...]) * sc_ref[...]


def _pool_grid_kernel(x_ref, w_ref, sc_ref, o_ref, cs_ref):
    gw = GRID_W
    rows = x_ref.shape[0] // gw
    shape = (gw, D_GROUP)
    col = lax.broadcasted_iota(jnp.int32, shape, 0)
    lane = lax.broadcasted_iota(jnp.int32, shape, 1)
    half = _by_group([jnp.full(shape, h, jnp.int32) for h in POOL_HALF], lane)
    cnt_c = _by_group([_clipped_count(col, h, gw) for h in POOL_HALF], lane)

    def col_stage(r, _):
        at = pl.ds(pl.multiple_of(r * gw, gw), gw)
        cs_ref[at, :] = _by_group(_window_sums(x_ref[at, :], col, gw), lane)
        return 0

    lax.fori_loop(0, rows, col_stage, 0)

    def row_stage(r, _):
        at = pl.ds(pl.multiple_of(r * gw, gw), gw)
        acc = jnp.zeros(shape, F32)
        for o in range(-MAX_HALF, MAX_HALF):
            rr = r + o
            src = pl.ds(pl.multiple_of(jnp.clip(rr, 0, rows - 1) * gw, gw), gw)
            in_window = (half > o) if o >= 0 else (half >= -o)
            ok = jnp.logical_and(in_window, jnp.logical_and(rr >= 0, rr < rows))
            acc = acc + jnp.where(ok, cs_ref[src, :], 0.0)
        cnt = (_clipped_count(r, half, rows) * cnt_c).astype(F32)
        o_ref[at, :] = _mm(acc / cnt - x_ref[at, :], w_ref[...]) * sc_ref[...]
        return 0

    lax.fori_loop(0, rows, row_stage, 0)


def pool_mix(z, w, scale, bsz, t, grid):
    n = bsz * t
    col = ZB_OFF // ZB_W
    scratch = [pltpu.VMEM((t, D_GROUP), F32)] if grid else []
    return pl.pallas_call(
        _pool_grid_kernel if grid else _pool_seq_kernel,
        grid=(bsz,),
        in_specs=[pl.BlockSpec((t, ZB_W), lambda i: (i, col)), _const((D_GROUP, D_GROUP)), _const((1, D_GROUP))],
        out_specs=pl.BlockSpec((t, D_GROUP), lambda i: (i, 0)),
        out_shape=jax.ShapeDtypeStruct((n, D_GROUP), F32),
        scratch_shapes=scratch,
        compiler_params=_params("parallel"),
        name="pool_grid" if grid else "pool_seq",
    )(z, w, scale)


CONV_TILE = 64


def _conv_kernel(z_ref, dw_ref, b_ref, lg_ref, lb_ref, pw_ref, o_ref, pad_ref):
    t = z_ref.shape[0]
    rt = CONV_TILE
    edge = jnp.zeros((CONV_PAD, D_GROUP), F32)
    pad_ref[0:CONV_PAD, :] = edge
    pad_ref[t + CONV_PAD:t + 2 * CONV_PAD, :] = edge

    def glu(i, _):
        base = pl.multiple_of(i * rt, rt)
        zz = z_ref[pl.ds(base, rt), :]
        pad_ref[pl.ds(base + CONV_PAD, rt), :] = zz[:, :D_GROUP] * _sigmoid(zz[:, D_GROUP:])
        return 0

    lax.fori_loop(0, t // rt, glu, 0)

    def tile(i, _):
        base = pl.multiple_of(i * rt, rt)
        acc = jnp.broadcast_to(b_ref[...], (rt, D_GROUP))
        rows = rt + 2 * CONV_PAD
        win = pad_ref[pl.ds(base, rows), :]
        first = CONV_PAD - CONV_W // 2
        for sub in range(SUB):
            shifted = pltpu.roll(win, (rows - sub) % rows, axis=0) if sub else win
            for j in range(CONV_W):
                off = first + j
                if off % SUB == sub:
                    acc = acc + dw_ref[j:j + 1, :] * shifted[off - sub:off - sub + rt, :]
        mu = jnp.mean(acc, axis=-1, keepdims=True)
        var = jnp.mean(jnp.square(acc - mu), axis=-1, keepdims=True)
        y = (acc - mu) * lax.rsqrt(var + CONV_LN_EPS) * lg_ref[...] + lb_ref[...]
        o_ref[pl.ds(base, rt), :] = _mm(y * _sigmoid(y), pw_ref[...])
        return 0

    lax.fori_loop(0, t // rt, tile, 0)


def conv_mix(z, p, bsz, t):
    n = bsz * t
    col = ZC_OFF // ZC_W
    vec = _const((1, D_GROUP))
    return pl.pallas_call(
        _conv_kernel,
        grid=(bsz,),
        in_specs=[pl.BlockSpec((t, ZC_W), lambda i: (i, col)), _const((CONV_W, D_GROUP)), vec, vec, vec,
                  _const((D_GROUP, D_GROUP))],
        out_specs=pl.BlockSpec((t, D_GROUP), lambda i: (i, 0)),
        out_shape=jax.ShapeDtypeStruct((n, D_GROUP), F32),
        scratch_shapes=[pltpu.VMEM((t + 2 * CONV_PAD, D_GROUP), F32)],
        compiler_params=_params("parallel"),
        name="conv_mix",
    )(z, p['dw'], p['b'], p['ln_g'], p['ln_b'], p['pw'])


def _mlstm_prep_kernel(t, z_ref, zp_ref, zn_ref, zg_ref, cw_ref, gb_ref, q_o, k_o, g_o):
    i = pl.program_id(0)
    z = z_ref[...]
    prev, nxt = _neighbours(z, zp_ref[...], zn_ref[...], i, ROW_TILE, t)
    qk = cw_ref[0:1, :] * prev + cw_ref[1:2, :] * z + cw_ref[2:3, :] * nxt
    qk = qk * _sigmoid(qk)
    q_o[...] = qk[:, :D_GROUP]
    k_o[...] = qk[:, D_GROUP:] * (1.0 / math.sqrt(HEAD_DIM))
    gb = zg_ref[...] + gb_ref[...]
    lane = lax.broadcasted_iota(jnp.int32, gb.shape, 1)
    g_o[...] = jnp.where(lane < N_DIR * N_HEADS, gb, -_softplus(-gb))


def mlstm_prep(z, p, t):
    n = z.shape[0]
    col = ZD_OFF // ZD_QK_W
    prev, nxt = _halo_specs(ZD_QK_W, col, n, ROW_TILE)
    out = pl.BlockSpec((ROW_TILE, D_GROUP), lambda i: (i, 0))
    gates = pl.BlockSpec((ROW_TILE, ZG_W), lambda i: (i, ZG_OFF // ZG_W))
    return pl.pallas_call(
        functools.partial(_mlstm_prep_kernel, t),
        grid=(n // ROW_TILE,),
        in_specs=[pl.BlockSpec((ROW_TILE, ZD_QK_W), lambda i: (i, col)), prev, nxt, gates,
                  _const((3, ZD_QK_W)), _const((1, ZG_W))],
        out_specs=[out, out, pl.BlockSpec((ROW_TILE, ZG_W), lambda i: (i, 0))],
        out_shape=[jax.ShapeDtypeStruct((n, D_GROUP), F32)] * 2 + [jax.ShapeDtypeStruct((n, ZG_W), F32)],
        compiler_params=_params("parallel"),
        name="mlstm_prep",
    )(z, z, z, z, p['qk_conv'], p['gate_bias'])


def _mlstm_scan_kernel(has_init, cb, *refs):
    if has_init:
        c0_ref, n0_ref, m0_ref = refs[:3]
        refs = refs[3:]
    ins, (h0_ref, h1_ref, cfin_ref, nfin_ref, mfin_ref, c_ref, n_ref, m_ref) = refs[:8], refs[8:]
    L = CHUNK
    nh = N_DIR * N_HEADS
    i = pl.program_id(1)

    @pl.when(i == 0)
    def _():
        if has_init:
            c_ref[...] = c0_ref[0]
            n_ref[...] = n0_ref[0]
            m_ref[...] = m0_ref[0]
        else:
            c_ref[...] = jnp.zeros_like(c_ref)
            n_ref[...] = jnp.zeros_like(n_ref)
            m_ref[...] = jnp.zeros_like(m_ref)

    heads = [slice(h * HEAD_DIM, (h + 1) * HEAD_DIM) for h in range(N_HEADS)]
    pick = (lax.broadcasted_iota(jnp.int32, (nh, ZG_W), 0) == lax.broadcasted_iota(jnp.int32, (nh, ZG_W), 1)).astype(F32)
    blocks = []
    for d in range(N_DIR):
        reverse = d == 1
        q_ref, k_ref, v_ref, g_ref = ins[4 * d:4 * d + 4]
        incl = _tri(L, reverse, False)
        for c in range(cb):
            rows = _chunk_rows(c, cb, reverse)
            blocks.append(dict(d=d, c=c, incl=incl, tri_f=incl.astype(F32), last=0 if reverse else L - 1,
                               q=_bf(q_ref[rows, :]), k=k_ref[rows, :], v=v_ref[rows, :], g=g_ref[rows, :]))
    for u in blocks:
        u['bc_all'] = _mm_hi(u['tri_f'], u['g'])
    for u in blocks:
        u['src'] = u['g'] - pltpu.roll(u['bc_all'], ZG_W - nh, axis=1)
    for u in blocks:
        u['src_rows'] = _mm_nt(pick, u['src'], HI)
    inst = []
    for u in blocks:
        for h, hs in enumerate(heads):
            j = u['d'] * N_HEADS + h
            last = u['last']
            bcum = u['bc_all'][:, nh + j:nh + j + 1]
            b_last = bcum[last:last + 1, :]
            log_w = jnp.where(u['incl'], bcum + u['src_rows'][j:j + 1, :], -jnp.inf)
            inst.append(dict(d=u['d'], c=u['c'], h=h, j=j, last=last, q=u['q'][:, hs],
                             k=u['k'][:, hs], v=u['v'][:, hs], bcum=bcum, b_last=b_last, log_w=log_w,
                             gl=b_last - bcum + u['g'][:, j:j + 1]))

    for t in inst:
        t['a'] = jnp.max(t['log_w'], axis=1, keepdims=True)
        t['qk'] = _dot_nt(t['q'], _bf(t['k']))
    for t in inst:
        t['a_last'] = t['a'][t['last']:t['last'] + 1, :]
        t['p'] = jnp.exp(t['log_w'] - t['a'])
    for t in inst:
        t['we'] = jnp.exp(t['gl'] - t['a_last'])
        t['s_loc'] = t['qk'] * t['p']
    for t in inst:
        t['wk'] = t['we'] * t['k']
        t['den_loc'] = jnp.sum(t['s_loc'], axis=1, keepdims=True)
        t['num_loc'] = _dot(_bf(t['s_loc']), _bf(t['v']))
    for t in inst:
        t['c_loc'] = _dot_tn(_bf(t['v']), _bf(t['wk']))
        t['n_loc'] = jnp.sum(t['wk'], axis=0, keepdims=True)

    cs = [c_ref[j // N_HEADS, j % N_HEADS] for j in range(nh)]
    ns = [n_ref[j:j + 1, :] for j in range(nh)]
    ms = [m_ref[j:j + 1, 0:1] for j in range(nh)]
    for c in range(cb):
        now = [t for t in inst if t['c'] == c]
        for t in now:
            j = t['j']
            t['c_in'], t['n_in'], t['m_in'] = cs[j], ns[j], ms[j]
            t['m_new'] = jnp.maximum(t['b_last'] + ms[j], t['a_last'])
        for t in now:
            t['carry'] = jnp.exp(t['b_last'] + t['m_in'] - t['m_new'])
            t['fresh'] = jnp.exp(t['a_last'] - t['m_new'])
        for t in now:
            j = t['j']
            cs[j] = t['carry'] * cs[j] + t['fresh'] * t['c_loc']
            ns[j] = t['carry'] * ns[j] + t['fresh'] * t['n_loc']
            ms[j] = t['m_new']

    for t in inst:
        t['qc'] = _dot_nt(t['q'], _bf(t['c_in']))
        t['qn'] = jnp.sum(t['q'].astype(F32) * t['n_in'], axis=1, keepdims=True)
        t['m_t'] = jnp.maximum(t['bcum'] + t['m_in'], t['a'])
    for t in inst:
        t['scale'] = jnp.exp(t['a'] - t['m_t'])
        t['inter'] = jnp.exp(t['bcum'] + t['m_in'] - t['m_t'])
        t['floor'] = jnp.exp(-t['m_t'])
    for t in inst:
        t['num'] = t['scale'] * t['num_loc'] + t['inter'] * t['qc']
        t['den'] = jnp.maximum(jnp.abs(t['scale'] * t['den_loc'] + t['inter'] * t['qn']), t['floor'])
    outs = {}
    for t in inst:
        outs[(t['d'], t['c'], t['h'])] = t['num'] / t['den']
    for d, h_ref in enumerate((h0_ref, h1_ref)):
        for c in range(cb):
            h_ref[_chunk_rows(c, cb, d == 1), :] = jnp.concatenate([outs[(d, c, h)] for h in range(N_HEADS)], axis=1)
    for j in range(nh):
        c_ref[j // N_HEADS, j % N_HEADS] = cs[j]
        n_ref[j:j + 1, :] = ns[j]
        m_ref[j:j + 1, :] = jnp.broadcast_to(ms[j], (1, ZG_W))

    @pl.when(i == pl.num_programs(1) - 1)
    def _():
        cfin_ref[0] = c_ref[...]
        nfin_ref[0] = n_ref[...]
        mfin_ref[0] = m_ref[...]


def mlstm_scan(q, k, z, g, init, bsz, t):
    cb = SCAN_CB
    blk = cb * CHUNK
    assert t % blk == 0
    nc = t // blk
    nh = N_DIR * N_HEADS
    vcol = (ZD_OFF + 2 * D_GROUP) // D_GROUP

    def specs(row):
        return [pl.BlockSpec((blk, D_GROUP), lambda bi, i: (row(bi, i), 0)),
                pl.BlockSpec((blk, D_GROUP), lambda bi, i: (row(bi, i), 0)),
                pl.BlockSpec((blk, D_GROUP), lambda bi, i: (row(bi, i), vcol)),
                pl.BlockSpec((blk, ZG_W), lambda bi, i: (row(bi, i), 0))]

    fwd = specs(lambda bi, i: bi * nc + i)
    bwd = specs(lambda bi, i: bi * nc + nc - 1 - i)
    cst = pl.BlockSpec((1, N_DIR, N_HEADS, HEAD_DIM, HEAD_DIM), lambda bi, i: (bi, 0, 0, 0, 0))
    nst = pl.BlockSpec((1, nh, HEAD_DIM), lambda bi, i: (bi, 0, 0))
    mst = pl.BlockSpec((1, nh, ZG_W), lambda bi, i: (bi, 0, 0))
    has_init = init is not None
    args = (list(init) if has_init else []) + [q, k, z, g] * 2
    tok = jax.ShapeDtypeStruct((bsz * t, D_GROUP), F32)
    return pl.pallas_call(
        functools.partial(_mlstm_scan_kernel, has_init, cb),
        grid=(bsz, nc),
        in_specs=([cst, nst, mst] if has_init else []) + fwd + bwd,
        out_specs=[fwd[0], bwd[0], cst, nst, mst],
        out_shape=[tok, tok,
                   jax.ShapeDtypeStruct((bsz, N_DIR, N_HEADS, HEAD_DIM, HEAD_DIM), F32),
                   jax.ShapeDtypeStruct((bsz, nh, HEAD_DIM), F32),
                   jax.ShapeDtypeStruct((bsz, nh, ZG_W), F32)],
        scratch_shapes=[pltpu.VMEM((N_DIR, N_HEADS, HEAD_DIM, HEAD_DIM), F32),
                        pltpu.VMEM((nh, HEAD_DIM), F32),
                        pltpu.VMEM((nh, ZG_W), F32)],
        compiler_params=_params("parallel", "arbitrary"),
        name="mlstm_scan",
    )(*args)


def _mix_out_kernel(x_ref, mod_ref, y0_ref, y1_ref, bon_ref, g_ref, lng_ref, lnb_ref, yb_ref, yc_ref,
                    h0_ref, h1_ref, o_ref, hng_ref, w_ref, out_ref):
    ones = _head_ones()
    inv = 1.0 / HEAD_DIM
    y = y0_ref[...] + y1_ref[...] + bon_ref[...]
    mu = _mm_hi(y, ones) * inv
    yc = y - mu
    var = _mm_hi(yc * yc, ones) * inv
    ya = (yc * lax.rsqrt(var + RWKV_LN_EPS) * lng_ref[...] + lnb_ref[...]) * g_ref[...]
    hs = (h0_ref[...] + h1_ref[...]) * _sigmoid(o_ref[...])
    yd = hs * lax.rsqrt(_mm_hi(hs * hs, ones) * inv + EPS) * hng_ref[...]
    g = D_GROUP
    mixed = (_mm(ya, w_ref[0:g, :]) + _mm(yb_ref[...], w_ref[g:2 * g, :])
             + _mm(yc_ref[...], w_ref[2 * g:3 * g, :]) + _mm(yd, w_ref[3 * g:4 * g, :]))
    out_ref[...] = x_ref[...] + mod_ref[0, 2:3, :] * mixed


def mix_out(x, mod, ra, yb, yc, md, z, p, t):
    n = x.shape[0]
    tok = pl.BlockSpec((ROW_TILE, D_GROUP), lambda i: (i, 0))
    ocol = (ZD_OFF + 3 * D_GROUP) // D_GROUP
    vec = _const((1, D_GROUP))
    row = pl.BlockSpec((ROW_TILE, D_MODEL), lambda i: (i, 0))
    return pl.pallas_call(
        _mix_out_kernel,
        grid=(n // ROW_TILE,),
        in_specs=[row, _mod_spec(mod.shape[0] > 1, t), tok, tok, tok, tok, vec, vec, tok, tok, tok, tok,
                  pl.BlockSpec((ROW_TILE, D_GROUP), lambda i: (i, ocol)), vec, _const((D_MODEL, D_MODEL))],
        out_specs=row,
        out_shape=jax.ShapeDtypeStruct((n, D_MODEL), F32),
        compiler_params=_params("parallel"),
        name="mix_out",
    )(x, mod, ra['y0'], ra['y1'], ra['bon'], ra['g'], p['rwkv_ln_g'], p['rwkv_ln_b'], yb, yc,
      md['h0'], md['h1'], z, p['hn_g'], p['w_out'])


def _mlp_kernel(final, x_ref, mod_ref, g_ref, w1_ref, b1_ref, w2_ref, b2_ref, fg_ref, out_ref):
    x = x_ref[...]
    h = _rms(x, g_ref[...]) * (1.0 + mod_ref[0, 4:5, :]) + mod_ref[0, 3:4, :]
    a = jnp.maximum(jnp.dot(h.astype(BF16), w1_ref[...], preferred_element_type=F32) + b1_ref[...], 0.0)
    f = jnp.dot((a * a).astype(BF16), w2_ref[...], preferred_element_type=F32) + b2_ref[...]
    x = x + mod_ref[0, 5:6, :] * f
    out_ref[...] = _rms(x, fg_ref[...]) if final else x


def mlp(x, mod, p, final_g, t, final):
    n = x.shape[0]
    row = pl.BlockSpec((ROW_TILE, D_MODEL), lambda i: (i, 0))
    return pl.pallas_call(
        functools.partial(_mlp_kernel, final),
        grid=(n // ROW_TILE,),
        in_specs=[row, _mod_spec(mod.shape[0] > 1, t), _const((1, D_MODEL)), _const((D_MODEL, D_FF)),
                  _const((1, D_FF)), _const((D_FF, D_MODEL)), _const((1, D_MODEL)), _const((1, D_MODEL))],
        out_specs=row,
        out_shape=jax.ShapeDtypeStruct((n, D_MODEL), F32),
        compiler_params=_params("parallel"),
        name="mlp",
    )(x, mod, p['norm2_g'], p['w1'], p['b1'], p['w2'], p['b2'], final_g)


def _trunk_layer(x, mod, p, states, bsz, t, grid, final_g, final):
    z = in_proj(x, mod, p['norm1_g'], p['w_in'], t)
    r, kk, v, g, bon, lw0, lw1, kd0, kd1, b0, b1 = rwkv_prep(z, p['rwkv'], t)
    y0, y1, s_rwkv = rwkv_scan(r, kk, v, (lw0, lw1), (kd0, kd1), (b0, b1),
                               None if states is None else states[0], bsz, t)
    yb = pool_mix(z, p['pool_w'], p['pool_scale'], bsz, t, grid)
    yc = conv_mix(z, p['conv'], bsz, t)
    q, k, gates = mlstm_prep(z, p['mlstm'], t)
    h0, h1, s_c, s_n, s_m = mlstm_scan(q, k, z, gates, None if states is None else states[1:], bsz, t)
    x = mix_out(x, mod, dict(y0=y0, y1=y1, bon=bon, g=g), yb, yc, dict(h0=h0, h1=h1), z, p, t)
    x = mlp(x, mod, p, final_g, t, final)
    return x, (s_rwkv, s_c, s_n, s_m)


def _layer_params(l, norm1_g, norm2_g, w_in, w_out, rwkv_mu, rwkv_w0, rwkv_w_up, rwkv_a0, rwkv_a_up, rwkv_g_up,
                  rwkv_k_k, rwkv_k_a, rwkv_r_k, rwkv_ln_g, rwkv_ln_b, pool_w, pool_scale, conv_dw, conv_b,
                  conv_ln_g, conv_ln_b, conv_pw, mlstm_qk_conv, mlstm_i_bias, mlstm_f_bias, mlstm_hn_g,
                  mlp_w1, mlp_b1, mlp_w2, mlp_b2):
    row = lambda a: a[l].reshape(1, -1)
    w = w_in[l]
    a_w, b_w, c_w = 1024, 256, 512
    wa, wb, wc, wd = w[:, :a_w], w[:, a_w:a_w + b_w], w[:, a_w + b_w:a_w + b_w + c_w], w[:, a_w + b_w + c_w:]
    nh = N_DIR * N_HEADS
    qkv, gates, o = wd[:, :3 * D_GROUP], wd[:, 3 * D_GROUP:3 * D_GROUP + 2 * nh], wd[:, 3 * D_GROUP + 2 * nh:]
    w_cat = jnp.concatenate([wa, qkv, o, wc, wb, gates, jnp.zeros((D_MODEL, ZG_W - 2 * nh), F32)], axis=1)
    rank = rwkv_w_up.shape[2]
    zeros = jnp.zeros((N_DIR, rank, D_GROUP), F32)
    pool_bd = jax.scipy.linalg.block_diag(*[pool_w[l, g] for g in range(len(POOL_HALF))])
    gate_bias = jnp.concatenate([mlstm_i_bias[l], mlstm_f_bias[l], jnp.zeros((ZG_W - 2 * nh,), F32)]).reshape(1, ZG_W)
    return dict(
        norm1_g=row(norm1_g), norm2_g=row(norm2_g), w_in=w_cat.astype(BF16), w_out=w_out[l].astype(BF16),
        rwkv=dict(mu=row(rwkv_mu), k_k=row(rwkv_k_k), k_a=row(rwkv_k_a), r_k=row(rwkv_r_k),
                  w0=rwkv_w0[l].reshape(N_DIR, 1, D_GROUP), a0=rwkv_a0[l].reshape(N_DIR, 1, D_GROUP),
                  w_up=jnp.concatenate([rwkv_w_up[l], zeros], axis=1).astype(BF16),
                  a_up=jnp.concatenate([zeros, rwkv_a_up[l]], axis=1).astype(BF16),
                  g_up=rwkv_g_up[l].astype(BF16)),
        rwkv_ln_g=row(rwkv_ln_g), rwkv_ln_b=row(rwkv_ln_b),
        pool_w=pool_bd.astype(BF16), pool_scale=row(pool_scale),
        conv=dict(dw=conv_dw[l], b=row(conv_b), ln_g=row(conv_ln_g), ln_b=row(conv_ln_b), pw=conv_pw[l].astype(BF16)),
        mlstm=dict(qk_conv=mlstm_qk_conv[l], gate_bias=gate_bias),
        hn_g=row(mlstm_hn_g),
        w1=mlp_w1[l].astype(BF16), b1=row(mlp_b1), w2=mlp_w2[l].astype(BF16), b2=row(mlp_b2),
    )


def kernel(x_prompt, x_sample, c, state_rwkv, state_mlstm_C, state_mlstm_n, state_mlstm_m, c_ctx, w_mod, b_mod, norm1_g, norm2_g, w_in, w_out, rwkv_mu, rwkv_w0, rwkv_w_up, rwkv_a0, rwkv_a_up, rwkv_g_up, rwkv_k_k, rwkv_k_a, rwkv_r_k, rwkv_ln_g, rwkv_ln_b, pool_w, pool_scale, conv_dw, conv_b, conv_ln_g, conv_ln_b, conv_pw, mlstm_qk_conv, mlstm_i_bias, mlstm_f_bias, mlstm_hn_g, mlp_w1, mlp_b1, mlp_w2, mlp_b2, final_g):
    bp, tp, d = x_prompt.shape
    bs, ts, _ = x_sample.shape
    nh = N_DIR * N_HEADS
    cvec = jnp.concatenate([c_ctx[None, :], c, jnp.zeros((SUB - 1 - bs, d), F32)], axis=0)
    mods = modulation(cvec, w_mod, b_mod).reshape(DEPTH, SUB, 6, d)
    xp = x_prompt.reshape(bp * tp, d)
    xs = x_sample.reshape(bs * ts, d)
    fg = final_g.reshape(1, d)
    new = [[], [], [], []]
    for l in range(DEPTH):
        p = _layer_params(l, norm1_g, norm2_g, w_in, w_out, rwkv_mu, rwkv_w0, rwkv_w_up, rwkv_a0, rwkv_a_up,
                          rwkv_g_up, rwkv_k_k, rwkv_k_a, rwkv_r_k, rwkv_ln_g, rwkv_ln_b, pool_w, pool_scale,
                          conv_dw, conv_b, conv_ln_g, conv_ln_b, conv_pw, mlstm_qk_conv, mlstm_i_bias,
                          mlstm_f_bias, mlstm_hn_g, mlp_w1, mlp_b1, mlp_w2, mlp_b2)
        final = l == DEPTH - 1
        xp, st = _trunk_layer(xp, mods[l, 0:1], p, None, bp, tp, False, fg, final)
        for acc, s in zip(new, st):
            acc.append(s)
        cached = (state_rwkv[:, l], state_mlstm_C[:, l], state_mlstm_n[:, l].reshape(bs, nh, HEAD_DIM),
                  jnp.broadcast_to(state_mlstm_m[:, l].reshape(bs, nh, 1), (bs, nh, ZG_W)))
        xs, _ = _trunk_layer(xs, mods[l, 1:1 + bs], p, cached, bs, ts, True, fg, final)
    return (xp.reshape(bp, tp, d), xs.reshape(bs, ts, d),
            jnp.stack(new[0], axis=1), jnp.stack(new[1], axis=1),
            jnp.stack(new[2], axis=1).reshape(bp, DEPTH, N_DIR, N_HEADS, HEAD_DIM),
            jnp.stack(new[3], axis=1)[..., 0].reshape(bp, DEPTH, N_DIR, N_HEADS))
```

```python
import functools
import math

import jax
import jax.numpy as jnp
from jax import lax
from jax.experimental import pallas as pl
from jax.experimental.pallas import tpu as pltpu

F32 = jnp.float32
BF16 = jnp.bfloat16

D_MODEL = 1024
D_GROUP = 256
HEAD_DIM = 64
N_HEADS = 4
N_DIR = 2
DEPTH = 4
GRID_W = 64
POOL_HALF = (1, 2, 4, 8)
MAX_HALF = POOL_HALF[-1]
CONV_W = 31
CONV_PAD = 16
D_FF = 4 * D_MODEL
EPS = 1e-6
RWKV_LN_EPS = 64e-5
CONV_LN_EPS = 1e-5
CHUNK = 64
ROW_TILE = 256
SUB = 8
VMEM_LIMIT = 56 * 1024 * 1024

Z_W = 2944
ZA_W, ZD_QK_W, ZC_W, ZB_W, ZG_W = 1024, 512, 512, 256, 128
ZD_OFF, ZC_OFF, ZB_OFF, ZG_OFF = 1024, 2048, 2560, 2816


def _mm(a, b):
    return jnp.dot(a.astype(BF16), b.astype(BF16), preferred_element_type=F32)


def _split(x, parts):
    out = []
    for _ in range(parts):
        p = x.astype(BF16)
        out.append(p)
        x = x - p.astype(F32)
    return out


def _head_sum(x, ones):
    return sum(jnp.dot(p, ones, preferred_element_type=F32) for p in _split(x, 2))


def _cumsum_mm(tri, x):
    return sum(jnp.dot(tri, p, preferred_element_type=F32) for p in _split(x, 3))


def _tri(n, reverse, strict):
    t = lax.broadcasted_iota(jnp.int32, (n, n), 0)
    s = lax.broadcasted_iota(jnp.int32, (n, n), 1)
    if reverse:
        return (s > t) if strict else (s >= t)
    return (s < t) if strict else (s <= t)


def _head_ones():
    a = lax.broadcasted_iota(jnp.int32, (D_GROUP, D_GROUP), 0) // HEAD_DIM
    b = lax.broadcasted_iota(jnp.int32, (D_GROUP, D_GROUP), 1) // HEAD_DIM
    return (a == b).astype(BF16)


def _sigmoid(x):
    return jax.nn.sigmoid(x)


def _softplus(x):
    return jnp.maximum(x, 0.0) + jnp.log1p(jnp.exp(-jnp.abs(x)))


def _rms(x, g):
    return x * lax.rsqrt(jnp.mean(x * x, axis=-1, keepdims=True) + EPS) * g


def _params(*sem):
    return pltpu.CompilerParams(dimension_semantics=sem, vmem_limit_bytes=VMEM_LIMIT)


def _const(shape):
    return pl.BlockSpec(shape, lambda *_: (0,) * len(shape))


def _neighbours(z, prev_blk, next_blk, i, tt, t):
    row = lax.broadcasted_iota(jnp.int32, z.shape, 0)
    at_start = (i * tt) % t == 0
    at_end = ((i + 1) * tt) % t == 0
    first = jnp.where(at_start, 0.0, prev_blk[SUB - 1:SUB, :])
    last = jnp.where(at_end, 0.0, next_blk[0:1, :])
    prev = jnp.where(row == 0, first, pltpu.roll(z, 1, axis=0))
    nxt = jnp.where(row == tt - 1, last, pltpu.roll(z, tt - 1, axis=0))
    return prev, nxt


def _halo_specs(width, col, n_rows, tt):
    per = tt // SUB
    last = n_rows // SUB - 1
    prev = pl.BlockSpec((SUB, width), lambda i: (jnp.maximum(i * per - 1, 0), col))
    nxt = pl.BlockSpec((SUB, width), lambda i: (jnp.minimum((i + 1) * per, last), col))
    return prev, nxt


def _mod_kernel(c_ref, w_ref, b_ref, o_ref):
    c = c_ref[...]
    o_ref[0] = _mm(c * _sigmoid(c), w_ref[0]) + b_ref[0]


def modulation(cvec, w_mod, b_mod):
    tn = 1536
    n = w_mod.shape[-1]
    return pl.pallas_call(
        _mod_kernel,
        grid=(DEPTH, n // tn),
        in_specs=[pl.BlockSpec((SUB, D_MODEL), lambda l, j: (0, 0)),
                  pl.BlockSpec((1, D_MODEL, tn), lambda l, j: (l, 0, j)),
                  pl.BlockSpec((1, 1, tn), lambda l, j: (l, 0, j))],
        out_specs=pl.BlockSpec((1, SUB, tn), lambda l, j: (l, 0, j)),
        out_shape=jax.ShapeDtypeStruct((DEPTH, SUB, n), F32),
        compiler_params=_params("parallel", "parallel"),
        name="modulation",
    )(cvec, w_mod, b_mod.reshape(DEPTH, 1, n))


def _mod_spec(per_batch, rows_per_seq):
    if per_batch:
        return pl.BlockSpec((1, 6, D_MODEL), lambda i: (i * ROW_TILE // rows_per_seq, 0, 0))
    return pl.BlockSpec((1, 6, D_MODEL), lambda i: (0, 0, 0))


def _in_proj_kernel(x_ref, mod_ref, g_ref, w_ref, z_ref):
    h = _rms(x_ref[...], g_ref[...]) * (1.0 + mod_ref[0, 1:2, :]) + mod_ref[0, 0:1, :]
    z_ref[...] = jnp.dot(h.astype(BF16), w_ref[...], preferred_element_type=F32)


def in_proj(x, mod, g, w, t):
    n = x.shape[0]
    return pl.pallas_call(
        _in_proj_kernel,
        grid=(n // ROW_TILE,),
        in_specs=[pl.BlockSpec((ROW_TILE, D_MODEL), lambda i: (i, 0)),
                  _mod_spec(mod.shape[0] > 1, t),
                  _const((1, D_MODEL)),
                  _const((D_MODEL, Z_W))],
        out_specs=pl.BlockSpec((ROW_TILE, Z_W), lambda i: (i, 0)),
        out_shape=jax.ShapeDtypeStruct((n, Z_W), F32),
        compiler_params=_params("parallel"),
        name="in_proj",
    )(x, mod, g, w)


def _rwkv_prep_kernel(t, z_ref, zp_ref, zn_ref, mu_ref, kk_ref, ka_ref, rk_ref, w0_ref, a0_ref,
                      wup_ref, aup_ref, gup_ref,
                      r_o, kk_o, v_o, g_o, bon_o, lw0_o, lw1_o, kd0_o, kd1_o, b0_o, b1_o):
    i = pl.program_id(0)
    z = z_ref[...]
    prev, nxt = _neighbours(z, zp_ref[...], zn_ref[...], i, ROW_TILE, t)
    zm = z + (0.5 * (prev + nxt) - z) * mu_ref[...]
    r, k, v = zm[:, 0:256], zm[:, 256:512], zm[:, 512:768]
    lora = zm[:, 768:896]
    gd = zm[:, 896:1024]
    ones = _head_ones()
    kk = k * kk_ref[...]
    kk = kk * lax.rsqrt(_head_sum(kk * kk, ones) + 1e-12)
    wt = jnp.tanh(lora)
    r_o[...] = r
    kk_o[...] = kk
    v_o[...] = v
    g_o[...] = _mm(_sigmoid(gd), gup_ref[...])
    bon = jnp.zeros_like(r)
    for d, (lw_o, kd_o, b_o) in enumerate(((lw0_o, kd0_o, b0_o), (lw1_o, kd1_o, b1_o))):
        w_log = -_softplus(-(w0_ref[d] + _mm(wt, wup_ref[d]))) - 0.5
        lw_o[...] = -jnp.exp(w_log)
        a = _sigmoid(a0_ref[d] + _mm(lora, aup_ref[d]))
        kd = k * (1.0 + (a - 1.0) * ka_ref[...])
        kd_o[...] = kd
        b_o[...] = -a * kk
        bon = bon + _head_sum(r * kd * rk_ref[...], ones) * v
    bon_o[...] = bon


def rwkv_prep(z, p, t):
    n = z.shape[0]
    tile = pl.BlockSpec((ROW_TILE, ZA_W), lambda i: (i, 0))
    prev, nxt = _halo_specs(ZA_W, 0, n, ROW_TILE)
    out = pl.BlockSpec((ROW_TILE, D_GROUP), lambda i: (i, 0))
    vec = _const((1, D_GROUP))
    return pl.pallas_call(
        functools.partial(_rwkv_prep_kernel, t),
        grid=(n // ROW_TILE,),
        in_specs=[tile, prev, nxt, _const((1, ZA_W)), vec, vec, vec,
                  _const((N_DIR, 1, D_GROUP)), _const((N_DIR, 1, D_GROUP)),
                  _const((N_DIR, 128, D_GROUP)), _const((N_DIR, 128, D_GROUP)), _const((128, D_GROUP))],
        out_specs=[out] * 11,
        out_shape=[jax.ShapeDtypeStruct((n, D_GROUP), F32)] * 11,
        compiler_params=_params("parallel"),
        name="rwkv_prep",
    )(z, z, z, p['mu'], p['k_k'], p['k_a'], p['r_k'], p['w0'], p['a0'], p['w_up'], p['a_up'], p['g_up'])


SCAN_CB = 4


def _bf(x):
    return x.astype(BF16)


def _dot(a, b):
    return jnp.dot(a, b, preferred_element_type=F32)


def _dot_nt(a, b):
    return lax.dot_general(a, b, (((1,), (1,)), ((), ())), preferred_element_type=F32)


def _dot_tn(a, b):
    return lax.dot_general(a, b, (((0,), (0,)), ((), ())), preferred_element_type=F32)


def _chunk_rows(c, cb, reverse):
    return slice((cb - 1 - c) * CHUNK, (cb - c) * CHUNK) if reverse else slice(c * CHUNK, (c + 1) * CHUNK)


def _rwkv_scan_kernel(has_init, cb, *refs):
    if has_init:
        s0_ref, refs = refs[0], refs[1:]
    ins, (y0_ref, y1_ref, sfin_ref, s_ref) = refs[:12], refs[12:]
    L = CHUNK
    i = pl.program_id(1)

    @pl.when(i == 0)
    def _():
        if has_init:
            s_ref[...] = s0_ref[0]
        else:
            s_ref[...] = jnp.zeros_like(s_ref)

    eye = (lax.broadcasted_iota(jnp.int32, (L, L), 0) == lax.broadcasted_iota(jnp.int32, (L, L), 1)).astype(F32)
    heads = [slice(h * HEAD_DIM, (h + 1) * HEAD_DIM) for h in range(N_HEADS)]

    blocks = []
    for d in range(N_DIR):
        reverse = d == 1
        incl = _tri(L, reverse, False)
        for c in range(cb):
            rows = _chunk_rows(c, cb, reverse)
            blocks.append(dict(d=d, c=c, rows=rows, reverse=reverse, incl=incl, strict=_tri(L, reverse, True),
                               lw=ins[6 * d + 3][rows, :]))
    for u in blocks:
        u['cum'] = _cumsum_mm(u['incl'].astype(BF16), u['lw'])
    inst = []
    for u in blocks:
        d, rows, cum, lw = u['d'], u['rows'], u['cum'], u['lw']
        r_ref, kk_ref, v_ref, _, kd_ref, b_ref = ins[6 * d:6 * d + 6]
        r, kk, v, kd, b = (ref[rows, :] for ref in (r_ref, kk_ref, v_ref, kd_ref, b_ref))
        last = cum[0:1, :] if u['reverse'] else cum[L - 1:L, :]
        e_neg = jnp.exp(-cum)
        e_end = jnp.exp(last - cum)
        e_all = jnp.exp(last)
        aq = _bf(kk * jnp.exp(cum - lw))
        rq = r * jnp.exp(cum)
        bk = _bf(b * e_neg)
        kq = _bf(kd * e_neg)
        k_end = _bf(kd * e_end)
        b_end = _bf(b * e_end)
        rq_b, v_b = _bf(rq), _bf(v)
        for h, hs in enumerate(heads):
            inst.append(dict(d=d, c=u['c'], h=h, incl=u['incl'], strict=u['strict'], aq=aq[:, hs],
                             rq=rq[:, hs], rq_b=rq_b[:, hs], bk=bk[:, hs], kq=kq[:, hs], k_end=k_end[:, hs],
                             b_end=b_end[:, hs], v=v_b[:, hs], e_all=e_all[:, hs]))

    for t in inst:
        t['m'] = jnp.where(t['strict'], _dot_nt(t['aq'], t['bk']), 0.0)
        t['n'] = _bf(jnp.where(t['strict'], _dot_nt(t['aq'], t['kq']), 0.0))
        t['pb'] = _bf(jnp.where(t['incl'], _dot_nt(t['rq_b'], t['bk']), 0.0))
        t['pk'] = _bf(jnp.where(t['incl'], _dot_nt(t['rq_b'], t['kq']), 0.0))
    row = lax.broadcasted_iota(jnp.int32, (L, L), 0)
    col = lax.broadcasted_iota(jnp.int32, (L, L), 1)
    sizes = [2 ** e for e in range(int(math.log2(L)))]
    corner = {s: jnp.logical_and(row // (2 * s) == col // (2 * s), row // s != col // s) for s in sizes}
    for t in inst:
        t['y'] = eye + jnp.where(corner[1], t['m'], 0.0)
    for s in sizes[1:]:
        for t in inst:
            t['my'] = _bf(_dot(_bf(jnp.where(corner[s], t['m'], 0.0)), _bf(t['y'])))
        for t in inst:
            t['y'] = t['y'] + _dot(_bf(t['y']), t['my'])
    for t in inst:
        t['y'] = _bf(t['y'])
        t['nv'] = _bf(_dot(t['n'], t['v']))
    for t in inst:
        t['w'] = _bf(_dot(t['y'], t['nv']))
        t['q'] = _bf(_dot(t['y'], t['aq']))
    for t in inst:
        t['y_loc'] = _dot(t['pk'], t['v']) + _dot(t['pb'], t['w'])
        t['r2'] = _bf(t['rq'] + _dot(t['pb'], t['q']))
        t['tq'] = _bf(_dot_tn(t['q'], t['b_end']))
        t['g'] = _dot_tn(t['v'], t['k_end']) + _dot_tn(t['w'], t['b_end'])

    state = [[s_ref[d, h] for h in range(N_HEADS)] for d in range(N_DIR)]
    outs = {}
    for c in range(cb):
        now = [t for t in inst if t['c'] == c]
        for t in now:
            s0 = state[t['d']][t['h']]
            s0_b = _bf(s0)
            outs[(t['d'], c, t['h'])] = t['y_loc'] + _dot_nt(t['r2'], s0_b)
            t['s1'] = s0 * t['e_all'] + _dot(s0_b, t['tq']) + t['g']
        for t in now:
            state[t['d']][t['h']] = t['s1']
    for d, y_ref in enumerate((y0_ref, y1_ref)):
        for c in range(cb):
            y_ref[_chunk_rows(c, cb, d == 1), :] = jnp.concatenate([outs[(d, c, h)] for h in range(N_HEADS)], axis=1)
    for d in range(N_DIR):
        for h in range(N_HEADS):
            s_ref[d, h] = state[d][h]

    @pl.when(i == pl.num_programs(1) - 1)
    def _():
        sfin_ref[0] = s_ref[...]


def rwkv_scan(r, kk, v, lw, kd, b, s0, bsz, t):
    cb = SCAN_CB
    blk = cb * CHUNK
    assert t % blk == 0
    nc = t // blk
    fwd = pl.BlockSpec((blk, D_GROUP), lambda bi, i: (bi * nc + i, 0))
    bwd = pl.BlockSpec((blk, D_GROUP), lambda bi, i: (bi * nc + nc - 1 - i, 0))
    st = pl.BlockSpec((1, N_DIR, N_HEADS, HEAD_DIM, HEAD_DIM), lambda bi, i: (bi, 0, 0, 0, 0))
    has_init = s0 is not None
    args = ([s0] if has_init else []) + [r, kk, v, lw[0], kd[0], b[0], r, kk, v, lw[1], kd[1], b[1]]
    in_specs = ([st] if has_init else []) + [fwd] * 6 + [bwd] * 6
    tok = jax.ShapeDtypeStruct((bsz * t, D_GROUP), F32)
    return pl.pallas_call(
        functools.partial(_rwkv_scan_kernel, has_init, cb),
        grid=(bsz, nc),
        in_specs=in_specs,
        out_specs=[fwd, bwd, st],
        out_shape=[tok, tok, jax.ShapeDtypeStruct((bsz, N_DIR, N_HEADS, HEAD_DIM, HEAD_DIM), F32)],
        scratch_shapes=[pltpu.VMEM((N_DIR, N_HEADS, HEAD_DIM, HEAD_DIM), F32)],
        compiler_params=_params("parallel", "arbitrary"),
        name="rwkv_scan",
    )(*args)


def _shift_rows(x, k, pos, n):
    rows = x.shape[0]
    rolled = pltpu.roll(x, k % rows, axis=0)
    ok = (pos >= k) if k > 0 else (pos < n + k)
    return jnp.where(ok, rolled, 0.0)


def _window_sums(x, pos, n):
    fw = x
    bw = _shift_rows(x, 1, pos, n)
    out = [fw + bw]
    for h in POOL_HALF[:-1]:
        fw = fw + _shift_rows(fw, -h, pos, n)
        bw = bw + _shift_rows(bw, h, pos, n)
        out.append(fw + bw)
    return out


def _by_group(vals, lane):
    out = vals[-1]
    for g in range(len(vals) - 2, -1, -1):
        out = jnp.where(lane < (g + 1) * HEAD_DIM, vals[g], out)
    return out


def _clipped_count(pos, h, n):
    return jnp.minimum(pos + h, n) - jnp.maximum(pos - h, 0)


def _pool_seq_kernel(x_ref, o_ref):
    x = x_ref[...]
    t = x.shape[0]
    pos = lax.broadcasted_iota(jnp.int32, x.shape, 0)
    lane = lax.broadcasted_iota(jnp.int32, x.shape, 1)
    sums = _by_group(_window_sums(x, pos, t), lane)
    cnt = _by_group([_clipped_count(pos, h, t) for h in POOL_HALF], lane).astype(F32)
    o_ref[...] = sums / cnt - x


def _pool_grid_kernel(x_ref, o_ref, cs_ref):
    gw = GRID_W
    rows = x_ref.shape[0] // gw
    shape = (gw, D_GROUP)
    col = lax.broadcasted_iota(jnp.int32, shape, 0)
    lane = lax.broadcasted_iota(jnp.int32, shape, 1)
    half = _by_group([jnp.full(shape, h, jnp.int32) for h in POOL_HALF], lane)
    cnt_c = _by_group([_clipped_count(col, h, gw) for h in POOL_HALF], lane)

    def col_stage(r, _):
        at = pl.ds(pl.multiple_of(r * gw, gw), gw)
        cs_ref[at, :] = _by_group(_window_sums(x_ref[at, :], col, gw), lane)
        return 0

    lax.fori_loop(0, rows, col_stage, 0)

    def row_stage(r, _):
        at = pl.ds(pl.multiple_of(r * gw, gw), gw)
        acc = jnp.zeros(shape, F32)
        for o in range(-MAX_HALF, MAX_HALF):
            rr = r + o
            src = pl.ds(pl.multiple_of(jnp.clip(rr, 0, rows - 1) * gw, gw), gw)
            in_window = (half > o) if o >= 0 else (half >= -o)
            ok = jnp.logical_and(in_window, jnp.logical_and(rr >= 0, rr < rows))
            acc = acc + jnp.where(ok, cs_ref[src, :], 0.0)
        cnt = (_clipped_count(r, half, rows) * cnt_c).astype(F32)
        o_ref[at, :] = acc / cnt - x_ref[at, :]
        return 0

    lax.fori_loop(0, rows, row_stage, 0)


def pool_mix(z, bsz, t, grid):
    n = bsz * t
    col = ZB_OFF // ZB_W
    scratch = [pltpu.VMEM((t, D_GROUP), F32)] if grid else []
    return pl.pallas_call(
        _pool_grid_kernel if grid else _pool_seq_kernel,
        grid=(bsz,),
        in_specs=[pl.BlockSpec((t, ZB_W), lambda i: (i, col))],
        out_specs=pl.BlockSpec((t, D_GROUP), lambda i: (i, 0)),
        out_shape=jax.ShapeDtypeStruct((n, D_GROUP), F32),
        scratch_shapes=scratch,
        compiler_params=_params("parallel"),
        name="pool_grid" if grid else "pool_seq",
    )(z)


CONV_TILE = 64


def _conv_kernel(z_ref, dw_ref, b_ref, o_ref, pad_ref):
    t = z_ref.shape[0]
    rt = CONV_TILE
    edge = jnp.zeros((CONV_PAD, D_GROUP), F32)
    pad_ref[0:CONV_PAD, :] = edge
    pad_ref[t + CONV_PAD:t + 2 * CONV_PAD, :] = edge

    def glu(i, _):
        base = pl.multiple_of(i * rt, rt)
        zz = z_ref[pl.ds(base, rt), :]
        pad_ref[pl.ds(base + CONV_PAD, rt), :] = zz[:, :D_GROUP] * _sigmoid(zz[:, D_GROUP:])
        return 0

    lax.fori_loop(0, t // rt, glu, 0)

    def tile(i, _):
        base = pl.multiple_of(i * rt, rt)
        rows = rt + 2 * CONV_PAD
        first = CONV_PAD - CONV_W // 2
        halves = []
        for lanes in (slice(0, D_GROUP // 2), slice(D_GROUP // 2, D_GROUP)):
            acc = jnp.broadcast_to(b_ref[:, lanes], (rt, D_GROUP // 2))
            win = pad_ref[pl.ds(base, rows), lanes]
            for sub in range(SUB):
                shifted = pltpu.roll(win, (rows - sub) % rows, axis=0) if sub else win
                for j in range(CONV_W):
                    off = first + j
                    if off % SUB == sub:
                        acc = acc + dw_ref[j:j + 1, lanes] * shifted[off - sub:off - sub + rt, :]
            halves.append(acc)
        o_ref[pl.ds(base, rt), :] = jnp.concatenate(halves, axis=1)
        return 0

    lax.fori_loop(0, t // rt, tile, 0)


def conv_mix(z, p, bsz, t):
    n = bsz * t
    col = ZC_OFF // ZC_W
    vec = _const((1, D_GROUP))
    return pl.pallas_call(
        _conv_kernel,
        grid=(bsz,),
        in_specs=[pl.BlockSpec((t, ZC_W), lambda i: (i, col)), _const((CONV_W, D_GROUP)), vec],
        out_specs=pl.BlockSpec((t, D_GROUP), lambda i: (i, 0)),
        out_shape=jax.ShapeDtypeStruct((n, D_GROUP), F32),
        scratch_shapes=[pltpu.VMEM((t + 2 * CONV_PAD, D_GROUP), F32)],
        compiler_params=_params("parallel"),
        name="conv_mix",
    )(z, p['dw'], p['b'])


def _mlstm_prep_kernel(t, z_ref, zp_ref, zn_ref, zg_ref, cw_ref, gb_ref, q_o, k_o, g_o):
    i = pl.program_id(0)
    z = z_ref[...]
    prev, nxt = _neighbours(z, zp_ref[...], zn_ref[...], i, ROW_TILE, t)
    qk = cw_ref[0:1, :] * prev + cw_ref[1:2, :] * z + cw_ref[2:3, :] * nxt
    qk = qk * _sigmoid(qk)
    q_o[...] = qk[:, :D_GROUP]
    k_o[...] = qk[:, D_GROUP:] * (1.0 / math.sqrt(HEAD_DIM))
    gb = zg_ref[...] + gb_ref[...]
    lane = lax.broadcasted_iota(jnp.int32, gb.shape, 1)
    g_o[...] = jnp.where(lane < N_DIR * N_HEADS, gb, -_softplus(-gb))


def mlstm_prep(z, p, t):
    n = z.shape[0]
    col = ZD_OFF // ZD_QK_W
    prev, nxt = _halo_specs(ZD_QK_W, col, n, ROW_TILE)
    out = pl.BlockSpec((ROW_TILE, D_GROUP), lambda i: (i, 0))
    gates = pl.BlockSpec((ROW_TILE, ZG_W), lambda i: (i, ZG_OFF // ZG_W))
    return pl.pallas_call(
        functools.partial(_mlstm_prep_kernel, t),
        grid=(n // ROW_TILE,),
        in_specs=[pl.BlockSpec((ROW_TILE, ZD_QK_W), lambda i: (i, col)), prev, nxt, gates,
                  _const((3, ZD_QK_W)), _const((1, ZG_W))],
        out_specs=[out, out, pl.BlockSpec((ROW_TILE, ZG_W), lambda i: (i, 0))],
        out_shape=[jax.ShapeDtypeStruct((n, D_GROUP), F32)] * 2 + [jax.ShapeDtypeStruct((n, ZG_W), F32)],
        compiler_params=_params("parallel"),
        name="mlstm_prep",
    )(z, z, z, z, p['qk_conv'], p['gate_bias'])


def _lane_group(n_lanes):
    return lax.broadcasted_iota(jnp.int32, (1, n_lanes), 1) // HEAD_DIM


def _mlstm_scan_kernel(has_init, cb, *refs):
    if has_init:
        c0_ref, n0_ref, m0_ref = refs[:3]
        refs = refs[3:]
    ins, (h0_ref, h1_ref, cfin_ref, nfin_ref, mfin_ref, c_ref, n_ref, m_ref) = refs[:8], refs[8:]
    L = CHUNK
    nh = N_DIR * N_HEADS
    W = D_GROUP
    i = pl.program_id(1)

    @pl.when(i == 0)
    def _():
        if has_init:
            c_ref[...] = c0_ref[0]
            n_ref[...] = n0_ref[0]
            m_ref[...] = m0_ref[0]
        else:
            c_ref[...] = jnp.zeros_like(c_ref)
            n_ref[...] = jnp.zeros_like(n_ref)
            m_ref[...] = jnp.zeros_like(m_ref)

    ones = _head_ones()
    row_head = lax.broadcasted_iota(jnp.int32, (W, 1), 0) // HEAD_DIM
    same_head = row_head == _lane_group(W)
    gate_lane = lax.broadcasted_iota(jnp.int32, (ZG_W, 1), 0)
    pick = (lax.broadcasted_iota(jnp.int32, (nh, ZG_W), 0) == lax.broadcasted_iota(jnp.int32, (nh, ZG_W), 1)).astype(BF16)
    s_pos = lax.broadcasted_iota(jnp.int32, (L, W), 1) % L
    t_pos = lax.broadcasted_iota(jnp.int32, (L, W), 0)
    lane128 = lax.broadcasted_iota(jnp.int32, (L, 128), 1)

    blocks = []
    for d in range(N_DIR):
        reverse = d == 1
        q_ref, k_ref, v_ref, g_ref = ins[4 * d:4 * d + 4]
        spread_i = (gate_lane == d * N_HEADS + _lane_group(W)).astype(BF16)
        spread_b = (gate_lane == nh + d * N_HEADS + _lane_group(W)).astype(BF16)
        for c in range(cb):
            rows = _chunk_rows(c, cb, reverse)
            blocks.append(dict(d=d, c=c, reverse=reverse, last=0 if reverse else L - 1,
                               causal=(s_pos >= t_pos) if reverse else (s_pos <= t_pos),
                               tri=_tri(L, reverse, False).astype(BF16), spread_i=spread_i, spread_b=spread_b,
                               q=_bf(q_ref[rows, :]), k=k_ref[rows, :], v=_bf(v_ref[rows, :]), g=g_ref[rows, :]))
    for u in blocks:
        u['bc_all'] = _cumsum_mm(u['tri'], u['g'])
    for u in blocks:
        u['src'] = u['g'] - pltpu.roll(u['bc_all'], ZG_W - nh, axis=1)
        u['bb'] = sum(_dot(p, u['spread_b']) for p in _split(u['bc_all'], 3))
        u['li'] = sum(_dot(p, u['spread_i']) for p in _split(u['g'], 3))
    for u in blocks:
        rows8 = sum(_dot_nt(pick, p) for p in _split(u['src'], 3))
        j0 = u['d'] * N_HEADS
        u['src_row'] = jnp.concatenate([rows8[j0 + h:j0 + h + 1, :] for h in range(N_HEADS)], axis=1)
        last = u['last']
        u['b_last'] = u['bb'][last:last + 1, :]
        k4 = jnp.concatenate([_bf(u['k'])] * N_HEADS, axis=0)
        v4 = jnp.concatenate([u['v']] * N_HEADS, axis=0)
        u['k_exp'] = jnp.where(same_head, k4, jnp.zeros_like(k4))
        u['v_exp'] = jnp.where(same_head, v4, jnp.zeros_like(v4))
    for u in blocks:
        u['log_w'] = jnp.where(u['causal'], u['bb'] + u['src_row'], -jnp.inf)
        u['qk'] = _dot_nt(u['q'], u['k_exp'])
        u['gl'] = u['b_last'] - u['bb'] + u['li']
    for u in blocks:
        cols = []
        for half in range(2):
            x = u['log_w'][:, half * 128:(half + 1) * 128]
            lo = jnp.max(jnp.where(lane128 < HEAD_DIM, x, -jnp.inf), axis=1, keepdims=True)
            hi = jnp.max(jnp.where(lane128 >= HEAD_DIM, x, -jnp.inf), axis=1, keepdims=True)
            cols.append(jnp.where(lane128 < HEAD_DIM, lo, hi))
        u['a'] = jnp.concatenate(cols, axis=1)
    for u in blocks:
        last = u['last']
        u['a_last'] = u['a'][last:last + 1, :]
        u['s_loc'] = u['qk'] * jnp.exp(u['log_w'] - u['a'])
    for u in blocks:
        u['wk'] = jnp.exp(u['gl'] - u['a_last']) * u['k']
        u['den_loc'] = _head_sum(u['s_loc'], ones)
        u['num_loc'] = _dot(_bf(u['s_loc']), u['v_exp'])
    for u in blocks:
        u['c_loc'] = jnp.where(same_head, _dot_tn(u['v'], _bf(u['wk'])), 0.0)
        u['n_loc'] = jnp.sum(u['wk'], axis=0, keepdims=True)

    cs = [c_ref[d] for d in range(N_DIR)]
    ns = [n_ref[d:d + 1, :] for d in range(N_DIR)]
    ms = [m_ref[d:d + 1, :] for d in range(N_DIR)]
    for c in range(cb):
        now = [u for u in blocks if u['c'] == c]
        for u in now:
            d = u['d']
            u['c_in'], u['n_in'], u['m_in'] = cs[d], ns[d], ms[d]
            u['m_new'] = jnp.maximum(u['b_last'] + ms[d], u['a_last'])
        for u in now:
            u['carry'] = jnp.exp(u['b_last'] + u['m_in'] - u['m_new'])
            u['fresh'] = jnp.exp(u['a_last'] - u['m_new'])
        for u in now:
            d = u['d']
            cs[d] = u['carry'] * cs[d] + u['fresh'] * u['c_loc']
            ns[d] = u['carry'] * ns[d] + u['fresh'] * u['n_loc']
            ms[d] = u['m_new']

    for u in blocks:
        u['qc'] = _dot_nt(u['q'], _bf(u['c_in']))
        u['qn'] = _head_sum(u['q'].astype(F32) * u['n_in'], ones)
        u['m_t'] = jnp.maximum(u['bb'] + u['m_in'], u['a'])
    for u in blocks:
        u['scale'] = jnp.exp(u['a'] - u['m_t'])
        u['inter'] = jnp.exp(u['bb'] + u['m_in'] - u['m_t'])
        u['floor'] = jnp.exp(-u['m_t'])
    for u in blocks:
        u['num'] = u['scale'] * u['num_loc'] + u['inter'] * u['qc']
        u['den'] = jnp.maximum(jnp.abs(u['scale'] * u['den_loc'] + u['inter'] * u['qn']), u['floor'])
    for u in blocks:
        h_ref = h1_ref if u['reverse'] else h0_ref
        h_ref[_chunk_rows(u['c'], cb, u['reverse']), :] = u['num'] / u['den']
    for d in range(N_DIR):
        c_ref[d] = cs[d]
        n_ref[d:d + 1, :] = ns[d]
        m_ref[d:d + 1, :] = ms[d]

    @pl.when(i == pl.num_programs(1) - 1)
    def _():
        cfin_ref[0] = c_ref[...]
        nfin_ref[0] = n_ref[...]
        mfin_ref[0] = m_ref[...]


def mlstm_scan(q, k, z, g, init, bsz, t):
    cb = SCAN_CB
    blk = cb * CHUNK
    assert t % blk == 0
    nc = t // blk
    vcol = (ZD_OFF + 2 * D_GROUP) // D_GROUP

    def specs(row):
        return [pl.BlockSpec((blk, D_GROUP), lambda bi, i: (row(bi, i), 0)),
                pl.BlockSpec((blk, D_GROUP), lambda bi, i: (row(bi, i), 0)),
                pl.BlockSpec((blk, D_GROUP), lambda bi, i: (row(bi, i), vcol)),
                pl.BlockSpec((blk, ZG_W), lambda bi, i: (row(bi, i), 0))]

    fwd = specs(lambda bi, i: bi * nc + i)
    bwd = specs(lambda bi, i: bi * nc + nc - 1 - i)
    cst = pl.BlockSpec((1, N_DIR, D_GROUP, D_GROUP), lambda bi, i: (bi, 0, 0, 0))
    vst = pl.BlockSpec((1, N_DIR, D_GROUP), lambda bi, i: (bi, 0, 0))
    has_init = init is not None
    args = (list(init) if has_init else []) + [q, k, z, g] * 2
    tok = jax.ShapeDtypeStruct((bsz * t, D_GROUP), F32)
    vec = jax.ShapeDtypeStruct((bsz, N_DIR, D_GROUP), F32)
    return pl.pallas_call(
        functools.partial(_mlstm_scan_kernel, has_init, cb),
        grid=(bsz, nc),
        in_specs=([cst, vst, vst] if has_init else []) + fwd + bwd,
        out_specs=[fwd[0], bwd[0], cst, vst, vst],
        out_shape=[tok, tok, jax.ShapeDtypeStruct((bsz, N_DIR, D_GROUP, D_GROUP), F32), vec, vec],
        scratch_shapes=[pltpu.VMEM((N_DIR, D_GROUP, D_GROUP), F32),
                        pltpu.VMEM((N_DIR, D_GROUP), F32),
                        pltpu.VMEM((N_DIR, D_GROUP), F32)],
        compiler_params=_params("parallel", "arbitrary"),
        name="mlstm_scan",
    )(*args)


def _heads_to_block_diag(c):
    eye = jnp.eye(N_HEADS, dtype=c.dtype)
    out = c[..., :, :, None, :] * eye[:, None, :, None]
    return out.reshape(c.shape[:-3] + (D_GROUP, D_GROUP))


def _block_diag_to_heads(c):
    c = c.reshape(c.shape[:-2] + (N_HEADS, HEAD_DIM, N_HEADS, HEAD_DIM))
    return jnp.stack([c[..., h, :, h, :] for h in range(N_HEADS)], axis=-3)


def _mix_out_kernel(x_ref, mod_ref, y0_ref, y1_ref, bon_ref, g_ref, lng_ref, lnb_ref, pd_ref, pw_ref, ps_ref,
                    cv_ref, clg_ref, clb_ref, cpw_ref, h0_ref, h1_ref, o_ref, hng_ref, w_ref, out_ref):
    ones = _head_ones()
    inv = 1.0 / HEAD_DIM
    y = y0_ref[...] + y1_ref[...] + bon_ref[...]
    mu = _head_sum(y, ones) * inv
    yc = y - mu
    var = _head_sum(yc * yc, ones) * inv
    ya = (yc * lax.rsqrt(var + RWKV_LN_EPS) * lng_ref[...] + lnb_ref[...]) * g_ref[...]
    y_pool = _mm(pd_ref[...], pw_ref[...]) * ps_ref[...]
    cv = cv_ref[...]
    cmu = jnp.mean(cv, axis=-1, keepdims=True)
    cvar = jnp.mean(jnp.square(cv - cmu), axis=-1, keepdims=True)
    cn = (cv - cmu) * lax.rsqrt(cvar + CONV_LN_EPS) * clg_ref[...] + clb_ref[...]
    y_conv = _mm(cn * _sigmoid(cn), cpw_ref[...])
    hs = (h0_ref[...] + h1_ref[...]) * _sigmoid(o_ref[...])
    yd = hs * lax.rsqrt(_head_sum(hs * hs, ones) * inv + EPS) * hng_ref[...]
    g = D_GROUP
    mixed = (_mm(ya, w_ref[0:g, :]) + _mm(y_pool, w_ref[g:2 * g, :])
             + _mm(y_conv, w_ref[2 * g:3 * g, :]) + _mm(yd, w_ref[3 * g:4 * g, :]))
    out_ref[...] = x_ref[...] + mod_ref[0, 2:3, :] * mixed


def mix_out(x, mod, ra, pooled, conv, md, z, p, t):
    n = x.shape[0]
    tok = pl.BlockSpec((ROW_TILE, D_GROUP), lambda i: (i, 0))
    ocol = (ZD_OFF + 3 * D_GROUP) // D_GROUP
    vec = _const((1, D_GROUP))
    sq = _const((D_GROUP, D_GROUP))
    row = pl.BlockSpec((ROW_TILE, D_MODEL), lambda i: (i, 0))
    return pl.pallas_call(
        _mix_out_kernel,
        grid=(n // ROW_TILE,),
        in_specs=[row, _mod_spec(mod.shape[0] > 1, t), tok, tok, tok, tok, vec, vec, tok, sq, vec,
                  tok, vec, vec, sq, tok, tok,
                  pl.BlockSpec((ROW_TILE, D_GROUP), lambda i: (i, ocol)), vec, _const((D_MODEL, D_MODEL))],
        out_specs=row,
        out_shape=jax.ShapeDtypeStruct((n, D_MODEL), F32),
        compiler_params=_params("parallel"),
        name="mix_out",
    )(x, mod, ra['y0'], ra['y1'], ra['bon'], ra['g'], p['rwkv_ln_g'], p['rwkv_ln_b'],
      pooled, p['pool_w'], p['pool_scale'], conv, p['conv']['ln_g'], p['conv']['ln_b'], p['conv']['pw'],
      md['h0'], md['h1'], z, p['hn_g'], p['w_out'])


def _mlp_kernel(final, x_ref, mod_ref, g_ref, w1_ref, b1_ref, w2_ref, b2_ref, fg_ref, out_ref):
    x = x_ref[...]
    h = _rms(x, g_ref[...]) * (1.0 + mod_ref[0, 4:5, :]) + mod_ref[0, 3:4, :]
    a = jnp.maximum(jnp.dot(h.astype(BF16), w1_ref[...], preferred_element_type=F32) + b1_ref[...], 0.0)
    f = jnp.dot((a * a).astype(BF16), w2_ref[...], preferred_element_type=F32) + b2_ref[...]
    x = x + mod_ref[0, 5:6, :] * f
    out_ref[...] = _rms(x, fg_ref[...]) if final else x


def mlp(x, mod, p, final_g, t, final):
    n = x.shape[0]
    row = pl.BlockSpec((ROW_TILE, D_MODEL), lambda i: (i, 0))
    return pl.pallas_call(
        functools.partial(_mlp_kernel, final),
        grid=(n // ROW_TILE,),
        in_specs=[row, _mod_spec(mod.shape[0] > 1, t), _const((1, D_MODEL)), _const((D_MODEL, D_FF)),
                  _const((1, D_FF)), _const((D_FF, D_MODEL)), _const((1, D_MODEL)), _const((1, D_MODEL))],
        out_specs=row,
        out_shape=jax.ShapeDtypeStruct((n, D_MODEL), F32),
        compiler_params=_params("parallel"),
        name="mlp",
    )(x, mod, p['norm2_g'], p['w1'], p['b1'], p['w2'], p['b2'], final_g)


def _trunk_layer(x, mod, p, states, bsz, t, grid, final_g, final):
    z = in_proj(x, mod, p['norm1_g'], p['w_in'], t)
    r, kk, v, g, bon, lw0, lw1, kd0, kd1, b0, b1 = rwkv_prep(z, p['rwkv'], t)
    y0, y1, s_rwkv = rwkv_scan(r, kk, v, (lw0, lw1), (kd0, kd1), (b0, b1),
                               None if states is None else states[0], bsz, t)
    yb = pool_mix(z, bsz, t, grid)
    yc = conv_mix(z, p['conv'], bsz, t)
    q, k, gates = mlstm_prep(z, p['mlstm'], t)
    h0, h1, s_c, s_n, s_m = mlstm_scan(q, k, z, gates, None if states is None else states[1:], bsz, t)
    x = mix_out(x, mod, dict(y0=y0, y1=y1, bon=bon, g=g), yb, yc, dict(h0=h0, h1=h1), z, p, t)
    x = mlp(x, mod, p, final_g, t, final)
    return x, (s_rwkv, s_c, s_n, s_m)


def _layer_params(l, norm1_g, norm2_g, w_in, w_out, rwkv_mu, rwkv_w0, rwkv_w_up, rwkv_a0, rwkv_a_up, rwkv_g_up,
                  rwkv_k_k, rwkv_k_a, rwkv_r_k, rwkv_ln_g, rwkv_ln_b, pool_w, pool_scale, conv_dw, conv_b,
                  conv_ln_g, conv_ln_b, conv_pw, mlstm_qk_conv, mlstm_i_bias, mlstm_f_bias, mlstm_hn_g,
                  mlp_w1, mlp_b1, mlp_w2, mlp_b2):
    row = lambda a: a[l].reshape(1, -1)
    w = w_in[l]
    a_w, b_w, c_w = 1024, 256, 512
    wa, wb, wc, wd = w[:, :a_w], w[:, a_w:a_w + b_w], w[:, a_w + b_w:a_w + b_w + c_w], w[:, a_w + b_w + c_w:]
    nh = N_DIR * N_HEADS
    qkv, gates, o = wd[:, :3 * D_GROUP], wd[:, 3 * D_GROUP:3 * D_GROUP + 2 * nh], wd[:, 3 * D_GROUP + 2 * nh:]
    w_cat = jnp.concatenate([wa, qkv, o, wc, wb, gates, jnp.zeros((D_MODEL, ZG_W - 2 * nh), F32)], axis=1)
    rank = rwkv_w_up.shape[2]
    zeros = jnp.zeros((N_DIR, rank, D_GROUP), F32)
    pool_bd = jax.scipy.linalg.block_diag(*[pool_w[l, g] for g in range(len(POOL_HALF))])
    gate_bias = jnp.concatenate([mlstm_i_bias[l], mlstm_f_bias[l], jnp.zeros((ZG_W - 2 * nh,), F32)]).reshape(1, ZG_W)
    return dict(
        norm1_g=row(norm1_g), norm2_g=row(norm2_g), w_in=w_cat.astype(BF16), w_out=w_out[l].astype(BF16),
        rwkv=dict(mu=row(rwkv_mu), k_k=row(rwkv_k_k), k_a=row(rwkv_k_a), r_k=row(rwkv_r_k),
                  w0=rwkv_w0[l].reshape(N_DIR, 1, D_GROUP), a0=rwkv_a0[l].reshape(N_DIR, 1, D_GROUP),
                  w_up=jnp.concatenate([rwkv_w_up[l], zeros], axis=1).astype(BF16),
                  a_up=jnp.concatenate([zeros, rwkv_a_up[l]], axis=1).astype(BF16),
                  g_up=rwkv_g_up[l].astype(BF16)),
        rwkv_ln_g=row(rwkv_ln_g), rwkv_ln_b=row(rwkv_ln_b),
        pool_w=pool_bd.astype(BF16), pool_scale=row(pool_scale),
        conv=dict(dw=conv_dw[l], b=row(conv_b), ln_g=row(conv_ln_g), ln_b=row(conv_ln_b), pw=conv_pw[l].astype(BF16)),
        mlstm=dict(qk_conv=mlstm_qk_conv[l], gate_bias=gate_bias),
        hn_g=row(mlstm_hn_g),
        w1=mlp_w1[l].astype(BF16), b1=row(mlp_b1), w2=mlp_w2[l].astype(BF16), b2=row(mlp_b2),
    )


def kernel(x_prompt, x_sample, c, state_rwkv, state_mlstm_C, state_mlstm_n, state_mlstm_m, c_ctx, w_mod, b_mod, norm1_g, norm2_g, w_in, w_out, rwkv_mu, rwkv_w0, rwkv_w_up, rwkv_a0, rwkv_a_up, rwkv_g_up, rwkv_k_k, rwkv_k_a, rwkv_r_k, rwkv_ln_g, rwkv_ln_b, pool_w, pool_scale, conv_dw, conv_b, conv_ln_g, conv_ln_b, conv_pw, mlstm_qk_conv, mlstm_i_bias, mlstm_f_bias, mlstm_hn_g, mlp_w1, mlp_b1, mlp_w2, mlp_b2, final_g):
    bp, tp, d = x_prompt.shape
    bs, ts, _ = x_sample.shape
    nh = N_DIR * N_HEADS
    cvec = jnp.concatenate([c_ctx[None, :], c, jnp.zeros((SUB - 1 - bs, d), F32)], axis=0)
    mods = modulation(cvec, w_mod, b_mod).reshape(DEPTH, SUB, 6, d)
    xp = x_prompt.reshape(bp * tp, d)
    xs = x_sample.reshape(bs * ts, d)
    fg = final_g.reshape(1, d)
    new = [[], [], [], []]
    for l in range(DEPTH):
        p = _layer_params(l, norm1_g, norm2_g, w_in, w_out, rwkv_mu, rwkv_w0, rwkv_w_up, rwkv_a0, rwkv_a_up,
                          rwkv_g_up, rwkv_k_k, rwkv_k_a, rwkv_r_k, rwkv_ln_g, rwkv_ln_b, pool_w, pool_scale,
                          conv_dw, conv_b, conv_ln_g, conv_ln_b, conv_pw, mlstm_qk_conv, mlstm_i_bias,
                          mlstm_f_bias, mlstm_hn_g, mlp_w1, mlp_b1, mlp_w2, mlp_b2)
        final = l == DEPTH - 1
        xp, st = _trunk_layer(xp, mods[l, 0:1], p, None, bp, tp, False, fg, final)
        for acc, s in zip(new, st):
            acc.append(s)
        cached = (state_rwkv[:, l], _heads_to_block_diag(state_mlstm_C[:, l]),
                  state_mlstm_n[:, l].reshape(bs, N_DIR, D_GROUP),
                  jnp.repeat(state_mlstm_m[:, l], HEAD_DIM, axis=-1))
        xs, _ = _trunk_layer(xs, mods[l, 1:1 + bs], p, cached, bs, ts, True, fg, final)
    return (xp.reshape(bp, tp, d), xs.reshape(bs, ts, d),
            jnp.stack(new[0], axis=1), _block_diag_to_heads(jnp.stack(new[1], axis=1)),
            jnp.stack(new[2], axis=1).reshape(bp, DEPTH, N_DIR, N_HEADS, HEAD_DIM),
            jnp.stack(new[3], axis=1).reshape(bp, DEPTH, N_DIR, N_HEADS, HEAD_DIM)[..., 0])
```

```python
import functools
import math

import jax
import jax.numpy as jnp
from jax import lax
from jax.experimental import pallas as pl
from jax.experimental.pallas import tpu as pltpu

F32 = jnp.float32
BF16 = jnp.bfloat16

D_MODEL = 1024
D_GROUP = 256
HEAD_DIM = 64
N_HEADS = 4
N_DIR = 2
DEPTH = 4
GRID_W = 64
POOL_HALF = (1, 2, 4, 8)
MAX_HALF = POOL_HALF[-1]
CONV_W = 31
CONV_PAD = 16
D_FF = 4 * D_MODEL
EPS = 1e-6
RWKV_LN_EPS = 64e-5
CONV_LN_EPS = 1e-5
CHUNK = 64
ROW_TILE = 256
SUB = 8
VMEM_LIMIT = 56 * 1024 * 1024

Z_W = 2944
ZA_W, ZD_QK_W, ZC_W, ZB_W, ZG_W = 1024, 512, 512, 256, 128
ZD_OFF, ZC_OFF, ZB_OFF, ZG_OFF = 1024, 2048, 2560, 2816


def _mm(a, b):
    return jnp.dot(a.astype(BF16), b.astype(BF16), preferred_element_type=F32)


def _split(x, parts):
    out = []
    for _ in range(parts):
        p = x.astype(BF16)
        out.append(p)
        x = x - p.astype(F32)
    return out


def _head_sum(x, ones):
    return sum(jnp.dot(p, ones, preferred_element_type=F32) for p in _split(x, 2))


def _cumsum_mm(tri, x):
    return sum(jnp.dot(tri, p, preferred_element_type=F32) for p in _split(x, 3))


def _tri(n, reverse, strict):
    t = lax.broadcasted_iota(jnp.int32, (n, n), 0)
    s = lax.broadcasted_iota(jnp.int32, (n, n), 1)
    if reverse:
        return (s > t) if strict else (s >= t)
    return (s < t) if strict else (s <= t)


def _head_ones():
    a = lax.broadcasted_iota(jnp.int32, (D_GROUP, D_GROUP), 0) // HEAD_DIM
    b = lax.broadcasted_iota(jnp.int32, (D_GROUP, D_GROUP), 1) // HEAD_DIM
    return (a == b).astype(BF16)


def _sigmoid(x):
    return jax.nn.sigmoid(x)


def _softplus(x):
    return jnp.maximum(x, 0.0) + jnp.log1p(jnp.exp(-jnp.abs(x)))


def _rms(x, g):
    return x * lax.rsqrt(jnp.mean(x * x, axis=-1, keepdims=True) + EPS) * g


def _params(*sem):
    return pltpu.CompilerParams(dimension_semantics=sem, vmem_limit_bytes=VMEM_LIMIT)


def _const(shape):
    return pl.BlockSpec(shape, lambda *_: (0,) * len(shape), pipeline_mode=pl.Buffered(1))


def _neighbours(z, prev_blk, next_blk, i, tt, t):
    row = lax.broadcasted_iota(jnp.int32, z.shape, 0)
    at_start = (i * tt) % t == 0
    at_end = ((i + 1) * tt) % t == 0
    first = jnp.where(at_start, 0.0, prev_blk[SUB - 1:SUB, :])
    last = jnp.where(at_end, 0.0, next_blk[0:1, :])
    prev = jnp.where(row == 0, first, pltpu.roll(z, 1, axis=0))
    nxt = jnp.where(row == tt - 1, last, pltpu.roll(z, tt - 1, axis=0))
    return prev, nxt


def _halo_specs(width, col, n_rows, tt):
    per = tt // SUB
    last = n_rows // SUB - 1
    prev = pl.BlockSpec((SUB, width), lambda i: (jnp.maximum(i * per - 1, 0), col))
    nxt = pl.BlockSpec((SUB, width), lambda i: (jnp.minimum((i + 1) * per, last), col))
    return prev, nxt


def _mod_kernel(c_ref, w_ref, b_ref, o_ref):
    c = c_ref[...]
    o_ref[0] = _mm(c * _sigmoid(c), w_ref[0]) + b_ref[0]


def modulation(cvec, w_mod, b_mod):
    tn = 1536
    n = w_mod.shape[-1]
    return pl.pallas_call(
        _mod_kernel,
        grid=(DEPTH, n // tn),
        in_specs=[pl.BlockSpec((SUB, D_MODEL), lambda l, j: (0, 0)),
                  pl.BlockSpec((1, D_MODEL, tn), lambda l, j: (l, 0, j)),
                  pl.BlockSpec((1, 1, tn), lambda l, j: (l, 0, j))],
        out_specs=pl.BlockSpec((1, SUB, tn), lambda l, j: (l, 0, j)),
        out_shape=jax.ShapeDtypeStruct((DEPTH, SUB, n), F32),
        compiler_params=_params("parallel", "parallel"),
        name="modulation",
    )(cvec, w_mod, b_mod.reshape(DEPTH, 1, n))


def _mod_spec(per_batch, rows_per_seq):
    if per_batch:
        return pl.BlockSpec((1, 6, D_MODEL), lambda i: (i * ROW_TILE // rows_per_seq, 0, 0))
    return pl.BlockSpec((1, 6, D_MODEL), lambda i: (0, 0, 0))


def _in_proj_kernel(x_ref, mod_ref, g_ref, w_ref, z_ref):
    h = _rms(x_ref[...], g_ref[...]) * (1.0 + mod_ref[0, 1:2, :]) + mod_ref[0, 0:1, :]
    z_ref[...] = jnp.dot(h.astype(BF16), w_ref[...], preferred_element_type=F32)


def in_proj(x, mod, g, w, t):
    n = x.shape[0]
    return pl.pallas_call(
        _in_proj_kernel,
        grid=(n // ROW_TILE,),
        in_specs=[pl.BlockSpec((ROW_TILE, D_MODEL), lambda i: (i, 0)),
                  _mod_spec(mod.shape[0] > 1, t),
                  _const((1, D_MODEL)),
                  _const((D_MODEL, Z_W))],
        out_specs=pl.BlockSpec((ROW_TILE, Z_W), lambda i: (i, 0)),
        out_shape=jax.ShapeDtypeStruct((n, Z_W), F32),
        compiler_params=_params("parallel"),
        name="in_proj",
    )(x, mod, g, w)


def _rwkv_prep_kernel(t, z_ref, zp_ref, zn_ref, mu_ref, kk_ref, ka_ref, rk_ref, w0_ref, a0_ref,
                      wup_ref, aup_ref, gup_ref,
                      r_o, kk_o, v_o, g_o, bon_o, lw0_o, lw1_o, kd0_o, kd1_o, b0_o, b1_o):
    i = pl.program_id(0)
    z = z_ref[...]
    prev, nxt = _neighbours(z, zp_ref[...], zn_ref[...], i, ROW_TILE, t)
    zm = z + (0.5 * (prev + nxt) - z) * mu_ref[...]
    r, k, v = zm[:, 0:256], zm[:, 256:512], zm[:, 512:768]
    lora = zm[:, 768:896]
    gd = zm[:, 896:1024]
    ones = _head_ones()
    kk = k * kk_ref[...]
    kk = kk * lax.rsqrt(_head_sum(kk * kk, ones) + 1e-12)
    wt = jnp.tanh(lora)
    r_o[...] = r
    kk_o[...] = kk
    v_o[...] = v
    g_o[...] = _mm(_sigmoid(gd), gup_ref[...])
    bon = jnp.zeros_like(r)
    for d, (lw_o, kd_o, b_o) in enumerate(((lw0_o, kd0_o, b0_o), (lw1_o, kd1_o, b1_o))):
        lw_o[...] = -math.exp(-0.5) * _sigmoid(w0_ref[d] + _mm(wt, wup_ref[d]))
        a = _sigmoid(a0_ref[d] + _mm(lora, aup_ref[d]))
        kd = k * (1.0 + (a - 1.0) * ka_ref[...])
        kd_o[...] = kd
        b_o[...] = -a * kk
        bon = bon + _head_sum(r * kd * rk_ref[...], ones) * v
    bon_o[...] = bon


def rwkv_prep(z, p, t):
    n = z.shape[0]
    tile = pl.BlockSpec((ROW_TILE, ZA_W), lambda i: (i, 0))
    prev, nxt = _halo_specs(ZA_W, 0, n, ROW_TILE)
    out = pl.BlockSpec((ROW_TILE, D_GROUP), lambda i: (i, 0))
    vec = _const((1, D_GROUP))
    return pl.pallas_call(
        functools.partial(_rwkv_prep_kernel, t),
        grid=(n // ROW_TILE,),
        in_specs=[tile, prev, nxt, _const((1, ZA_W)), vec, vec, vec,
                  _const((N_DIR, 1, D_GROUP)), _const((N_DIR, 1, D_GROUP)),
                  _const((N_DIR, 128, D_GROUP)), _const((N_DIR, 128, D_GROUP)), _const((128, D_GROUP))],
        out_specs=[out] * 11,
        out_shape=[jax.ShapeDtypeStruct((n, D_GROUP), F32)] * 11,
        compiler_params=_params("parallel"),
        name="rwkv_prep",
    )(z, z, z, p['mu'], p['k_k'], p['k_a'], p['r_k'], p['w0'], p['a0'], p['w_up'], p['a_up'], p['g_up'])


SCAN_CB = 4


def _bf(x):
    return x.astype(BF16)


def _dot(a, b):
    return jnp.dot(a, b, preferred_element_type=F32)


def _dot_nt(a, b):
    return lax.dot_general(a, b, (((1,), (1,)), ((), ())), preferred_element_type=F32)


def _dot_tn(a, b):
    return lax.dot_general(a, b, (((0,), (0,)), ((), ())), preferred_element_type=F32)


def _tri2(reverse, strict):
    t = lax.broadcasted_iota(jnp.int32, (CHUNK, 2 * CHUNK), 0)
    s = lax.broadcasted_iota(jnp.int32, (CHUNK, 2 * CHUNK), 1) % CHUNK
    if reverse:
        return (s > t) if strict else (s >= t)
    return (s < t) if strict else (s <= t)


def _chunk_rows(c, cb, reverse):
    return slice((cb - 1 - c) * CHUNK, (cb - c) * CHUNK) if reverse else slice(c * CHUNK, (c + 1) * CHUNK)


def _rwkv_scan_kernel(has_init, cb, *refs):
    if has_init:
        s0_ref, refs = refs[0], refs[1:]
    ins, (y0_ref, y1_ref, sfin_ref, s_ref) = refs[:12], refs[12:]
    L = CHUNK
    i = pl.program_id(1)

    @pl.when(i == 0)
    def _():
        if has_init:
            s_ref[...] = s0_ref[0]
        else:
            s_ref[...] = jnp.zeros_like(s_ref)

    eye = (lax.broadcasted_iota(jnp.int32, (L, L), 0) == lax.broadcasted_iota(jnp.int32, (L, L), 1)).astype(F32)
    heads = [slice(h * HEAD_DIM, (h + 1) * HEAD_DIM) for h in range(N_HEADS)]

    blocks = []
    for d in range(N_DIR):
        reverse = d == 1
        incl = _tri(L, reverse, False)
        for c in range(cb):
            rows = _chunk_rows(c, cb, reverse)
            blocks.append(dict(d=d, c=c, rows=rows, reverse=reverse, incl=incl, lw=ins[6 * d + 3][rows, :]))
    for u in blocks:
        u['cum'] = _cumsum_mm(u['incl'].astype(BF16), u['lw'])
    inst = []
    for u in blocks:
        d, rows, cum, lw = u['d'], u['rows'], u['cum'], u['lw']
        r_ref, kk_ref, v_ref, _, kd_ref, b_ref = ins[6 * d:6 * d + 6]
        r, kk, v, kd, b = (ref[rows, :] for ref in (r_ref, kk_ref, v_ref, kd_ref, b_ref))
        last = cum[0:1, :] if u['reverse'] else cum[L - 1:L, :]
        e_neg = jnp.exp(-cum)
        e_end = jnp.exp(last - cum)
        e_all = jnp.exp(last)
        aq = _bf(kk * jnp.exp(cum - lw))
        rq = r * jnp.exp(cum)
        bk = _bf(b * e_neg)
        kq = _bf(kd * e_neg)
        k_end = _bf(kd * e_end)
        b_end = _bf(b * e_end)
        rq_b, v_b = _bf(rq), _bf(v)
        for h, hs in enumerate(heads):
            inst.append(dict(d=d, c=u['c'], h=h, reverse=u['reverse'], aq=aq[:, hs],
                             rq=rq[:, hs], rq_b=rq_b[:, hs], bk=bk[:, hs], kq=kq[:, hs], k_end=k_end[:, hs],
                             b_end=b_end[:, hs], v=v_b[:, hs], e_all=e_all[:, hs]))

    zeros_h = jnp.zeros((L, HEAD_DIM), BF16)
    for t in inst:
        res = _dot_nt(jnp.concatenate([t['aq'], t['rq_b']], axis=0), jnp.concatenate([t['bk'], t['kq']], axis=0))
        t['mn'] = jnp.where(_tri2(t['reverse'], True), res[:L], 0.0)
        t['pp'] = _bf(jnp.where(_tri2(t['reverse'], False), res[L:], 0.0))
        t['m'] = t['mn'][:, :L]
    row = lax.broadcasted_iota(jnp.int32, (L, L), 0)
    col = lax.broadcasted_iota(jnp.int32, (L, L), 1)
    sizes = [2 ** e for e in range(int(math.log2(L)))]
    corner = {s: jnp.logical_and(row // (2 * s) == col // (2 * s), row // s != col // s) for s in sizes}
    for t in inst:
        t['y'] = eye + jnp.where(corner[1], t['m'], 0.0)
    for s in sizes[1:]:
        for t in inst:
            t['my'] = _bf(_dot(_bf(jnp.where(corner[s], t['m'], 0.0)), _bf(t['y'])))
        for t in inst:
            t['y'] = t['y'] + _dot(_bf(t['y']), t['my'])
    for t in inst:
        t['y'] = _bf(t['y'])
        t['nv'] = _bf(_dot(_bf(t['mn']), jnp.concatenate([zeros_h, t['v']], axis=0)))
    for t in inst:
        t['w'] = _bf(_dot(t['y'], t['nv']))
        t['q'] = _bf(_dot(t['y'], t['aq']))
    for t in inst:
        t['y_loc'] = _dot(t['pp'], jnp.concatenate([t['w'], t['v']], axis=0))
        t['r2'] = _bf(t['rq'] + _dot(t['pp'], jnp.concatenate([t['q'], zeros_h], axis=0)))
        t['tq'] = _bf(_dot_tn(t['q'], t['b_end']))
        t['g'] = _dot_tn(jnp.concatenate([t['v'], t['w']], axis=0),
                         jnp.concatenate([t['k_end'], t['b_end']], axis=0))

    state = [[s_ref[d, h] for h in range(N_HEADS)] for d in range(N_DIR)]
    outs = {}
    for c in range(cb):
        now = [t for t in inst if t['c'] == c]
        for t in now:
            s0 = state[t['d']][t['h']]
            s0_b = _bf(s0)
            outs[(t['d'], c, t['h'])] = t['y_loc'] + _dot_nt(t['r2'], s0_b)
            t['s1'] = s0 * t['e_all'] + _dot(s0_b, t['tq']) + t['g']
        for t in now:
            state[t['d']][t['h']] = t['s1']
    for d, y_ref in enumerate((y0_ref, y1_ref)):
        for c in range(cb):
            y_ref[_chunk_rows(c, cb, d == 1), :] = jnp.concatenate([outs[(d, c, h)] for h in range(N_HEADS)], axis=1)
    for d in range(N_DIR):
        for h in range(N_HEADS):
            s_ref[d, h] = state[d][h]

    @pl.when(i == pl.num_programs(1) - 1)
    def _():
        sfin_ref[0] = s_ref[...]


def rwkv_scan(r, kk, v, lw, kd, b, s0, bsz, t):
    cb = SCAN_CB
    blk = cb * CHUNK
    assert t % blk == 0
    nc = t // blk
    fwd = pl.BlockSpec((blk, D_GROUP), lambda bi, i: (bi * nc + i, 0))
    bwd = pl.BlockSpec((blk, D_GROUP), lambda bi, i: (bi * nc + nc - 1 - i, 0))
    st = pl.BlockSpec((1, N_DIR, N_HEADS, HEAD_DIM, HEAD_DIM), lambda bi, i: (bi, 0, 0, 0, 0))
    has_init = s0 is not None
    args = ([s0] if has_init else []) + [r, kk, v, lw[0], kd[0], b[0], r, kk, v, lw[1], kd[1], b[1]]
    in_specs = ([st] if has_init else []) + [fwd] * 6 + [bwd] * 6
    tok = jax.ShapeDtypeStruct((bsz * t, D_GROUP), F32)
    return pl.pallas_call(
        functools.partial(_rwkv_scan_kernel, has_init, cb),
        grid=(bsz, nc),
        in_specs=in_specs,
        out_specs=[fwd, bwd, st],
        out_shape=[tok, tok, jax.ShapeDtypeStruct((bsz, N_DIR, N_HEADS, HEAD_DIM, HEAD_DIM), F32)],
        scratch_shapes=[pltpu.VMEM((N_DIR, N_HEADS, HEAD_DIM, HEAD_DIM), F32)],
        compiler_params=_params("parallel", "arbitrary"),
        name="rwkv_scan",
    )(*args)


def _shift_rows(x, k, pos, n):
    rows = x.shape[0]
    rolled = pltpu.roll(x, k % rows, axis=0)
    ok = (pos >= k) if k > 0 else (pos < n + k)
    return jnp.where(ok, rolled, 0.0)


def _window_sums(x, pos, n):
    fw = x
    bw = _shift_rows(x, 1, pos, n)
    out = [fw + bw]
    for h in POOL_HALF[:-1]:
        fw = fw + _shift_rows(fw, -h, pos, n)
        bw = bw + _shift_rows(bw, h, pos, n)
        out.append(fw + bw)
    return out


def _by_group(vals, lane):
    out = vals[-1]
    for g in range(len(vals) - 2, -1, -1):
        out = jnp.where(lane < (g + 1) * HEAD_DIM, vals[g], out)
    return out


def _clipped_count(pos, h, n):
    return jnp.minimum(pos + h, n) - jnp.maximum(pos - h, 0)


def _pool_seq_kernel(x_ref, o_ref):
    x = x_ref[...]
    t = x.shape[0]
    pos = lax.broadcasted_iota(jnp.int32, x.shape, 0)
    lane = lax.broadcasted_iota(jnp.int32, x.shape, 1)
    sums = _by_group(_window_sums(x, pos, t), lane)
    cnt = _by_group([_clipped_count(pos, h, t) for h in POOL_HALF], lane).astype(F32)
    o_ref[...] = sums / cnt - x


def _pool_grid_kernel(x_ref, o_ref, cs_ref):
    gw = GRID_W
    rows = x_ref.shape[0] // gw
    shape = (gw, D_GROUP)
    col = lax.broadcasted_iota(jnp.int32, shape, 0)
    lane = lax.broadcasted_iota(jnp.int32, shape, 1)
    half = _by_group([jnp.full(shape, h, jnp.int32) for h in POOL_HALF], lane)
    cnt_c = _by_group([_clipped_count(col, h, gw) for h in POOL_HALF], lane)

    def col_stage(r, _):
        at = pl.ds(pl.multiple_of(r * gw, gw), gw)
        cs_ref[at, :] = _by_group(_window_sums(x_ref[at, :], col, gw), lane)
        return 0

    lax.fori_loop(0, rows, col_stage, 0)

    def row_stage(r, _):
        at = pl.ds(pl.multiple_of(r * gw, gw), gw)
        acc = jnp.zeros(shape, F32)
        for o in range(-MAX_HALF, MAX_HALF):
            rr = r + o
            src = pl.ds(pl.multiple_of(jnp.clip(rr, 0, rows - 1) * gw, gw), gw)
            in_window = (half > o) if o >= 0 else (half >= -o)
            ok = jnp.logical_and(in_window, jnp.logical_and(rr >= 0, rr < rows))
            acc = acc + jnp.where(ok, cs_ref[src, :], 0.0)
        cnt = (_clipped_count(r, half, rows) * cnt_c).astype(F32)
        o_ref[at, :] = acc / cnt - x_ref[at, :]
        return 0

    lax.fori_loop(0, rows, row_stage, 0)


def pool_mix(z, bsz, t, grid):
    n = bsz * t
    col = ZB_OFF // ZB_W
    scratch = [pltpu.VMEM((t, D_GROUP), F32)] if grid else []
    return pl.pallas_call(
        _pool_grid_kernel if grid else _pool_seq_kernel,
        grid=(bsz,),
        in_specs=[pl.BlockSpec((t, ZB_W), lambda i: (i, col))],
        out_specs=pl.BlockSpec((t, D_GROUP), lambda i: (i, 0)),
        out_shape=jax.ShapeDtypeStruct((n, D_GROUP), F32),
        scratch_shapes=scratch,
        compiler_params=_params("parallel"),
        name="pool_grid" if grid else "pool_seq",
    )(z)


CONV_TILE = 64


def _conv_kernel(z_ref, dw_ref, b_ref, o_ref, pad_ref):
    t = z_ref.shape[0]
    rt = CONV_TILE
    edge = jnp.zeros((CONV_PAD, D_GROUP), F32)
    pad_ref[0:CONV_PAD, :] = edge
    pad_ref[t + CONV_PAD:t + 2 * CONV_PAD, :] = edge

    def glu(i, _):
        base = pl.multiple_of(i * rt, rt)
        zz = z_ref[pl.ds(base, rt), :]
        pad_ref[pl.ds(base + CONV_PAD, rt), :] = zz[:, :D_GROUP] * _sigmoid(zz[:, D_GROUP:])
        return 0

    lax.fori_loop(0, t // rt, glu, 0)

    def tile(i, _):
        base = pl.multiple_of(i * rt, rt)
        rows = rt + 2 * CONV_PAD
        first = CONV_PAD - CONV_W // 2
        halves = []
        for lanes in (slice(0, D_GROUP // 2), slice(D_GROUP // 2, D_GROUP)):
            acc = jnp.broadcast_to(b_ref[:, lanes], (rt, D_GROUP // 2))
            win = pad_ref[pl.ds(base, rows), lanes]
            for sub in range(SUB):
                shifted = pltpu.roll(win, (rows - sub) % rows, axis=0) if sub else win
                for j in range(CONV_W):
                    off = first + j
                    if off % SUB == sub:
                        acc = acc + dw_ref[j:j + 1, lanes] * shifted[off - sub:off - sub + rt, :]
            halves.append(acc)
        o_ref[pl.ds(base, rt), :] = jnp.concatenate(halves, axis=1)
        return 0

    lax.fori_loop(0, t // rt, tile, 0)


def conv_mix(z, p, bsz, t):
    n = bsz * t
    col = ZC_OFF // ZC_W
    vec = _const((1, D_GROUP))
    return pl.pallas_call(
        _conv_kernel,
        grid=(bsz,),
        in_specs=[pl.BlockSpec((t, ZC_W), lambda i: (i, col)), _const((CONV_W, D_GROUP)), vec],
        out_specs=pl.BlockSpec((t, D_GROUP), lambda i: (i, 0)),
        out_shape=jax.ShapeDtypeStruct((n, D_GROUP), F32),
        scratch_shapes=[pltpu.VMEM((t + 2 * CONV_PAD, D_GROUP), F32)],
        compiler_params=_params("parallel"),
        name="conv_mix",
    )(z, p['dw'], p['b'])


def _mlstm_prep_kernel(t, z_ref, zp_ref, zn_ref, zg_ref, cw_ref, gb_ref, q_o, k_o, g_o):
    i = pl.program_id(0)
    z = z_ref[...]
    prev, nxt = _neighbours(z, zp_ref[...], zn_ref[...], i, ROW_TILE, t)
    qk = cw_ref[0:1, :] * prev + cw_ref[1:2, :] * z + cw_ref[2:3, :] * nxt
    qk = qk * _sigmoid(qk)
    q_o[...] = qk[:, :D_GROUP]
    k_o[...] = qk[:, D_GROUP:] * (1.0 / math.sqrt(HEAD_DIM))
    gb = zg_ref[...] + gb_ref[...]
    lane = lax.broadcasted_iota(jnp.int32, gb.shape, 1)
    g_o[...] = jnp.where(lane < N_DIR * N_HEADS, gb, -_softplus(-gb))


def mlstm_prep(z, p, t):
    n = z.shape[0]
    col = ZD_OFF // ZD_QK_W
    prev, nxt = _halo_specs(ZD_QK_W, col, n, ROW_TILE)
    out = pl.BlockSpec((ROW_TILE, D_GROUP), lambda i: (i, 0))
    gates = pl.BlockSpec((ROW_TILE, ZG_W), lambda i: (i, ZG_OFF // ZG_W))
    return pl.pallas_call(
        functools.partial(_mlstm_prep_kernel, t),
        grid=(n // ROW_TILE,),
        in_specs=[pl.BlockSpec((ROW_TILE, ZD_QK_W), lambda i: (i, col)), prev, nxt, gates,
                  _const((3, ZD_QK_W)), _const((1, ZG_W))],
        out_specs=[out, out, pl.BlockSpec((ROW_TILE, ZG_W), lambda i: (i, 0))],
        out_shape=[jax.ShapeDtypeStruct((n, D_GROUP), F32)] * 2 + [jax.ShapeDtypeStruct((n, ZG_W), F32)],
        compiler_params=_params("parallel"),
        name="mlstm_prep",
    )(z, z, z, z, p['qk_conv'], p['gate_bias'])


def _lane_group(n_lanes):
    return lax.broadcasted_iota(jnp.int32, (1, n_lanes), 1) // HEAD_DIM


def _mlstm_scan_kernel(has_init, cb, *refs):
    if has_init:
        c0_ref, n0_ref, m0_ref = refs[:3]
        refs = refs[3:]
    ins, (h0_ref, h1_ref, cfin_ref, nfin_ref, mfin_ref, c_ref, n_ref, m_ref) = refs[:8], refs[8:]
    L = CHUNK
    nh = N_DIR * N_HEADS
    W = D_GROUP
    i = pl.program_id(1)

    @pl.when(i == 0)
    def _():
        c_ref[...] = jnp.zeros_like(c_ref)
        if has_init:
            for d in range(N_DIR):
                for h in range(N_HEADS):
                    hs = slice(h * HEAD_DIM, (h + 1) * HEAD_DIM)
                    c_ref[d, hs, hs] = c0_ref[0, d, h]
            n_ref[...] = n0_ref[0]
            m_ref[...] = m0_ref[0]
        else:
            n_ref[...] = jnp.zeros_like(n_ref)
            m_ref[...] = jnp.zeros_like(m_ref)

    ones = _head_ones()
    row_head = lax.broadcasted_iota(jnp.int32, (W, 1), 0) // HEAD_DIM
    same_head = row_head == _lane_group(W)
    gate_lane = lax.broadcasted_iota(jnp.int32, (ZG_W, 1), 0)
    pick = (lax.broadcasted_iota(jnp.int32, (nh, ZG_W), 0) == lax.broadcasted_iota(jnp.int32, (nh, ZG_W), 1)).astype(BF16)
    s_pos = lax.broadcasted_iota(jnp.int32, (L, W), 1) % L
    t_pos = lax.broadcasted_iota(jnp.int32, (L, W), 0)
    lane128 = lax.broadcasted_iota(jnp.int32, (L, 128), 1)

    blocks = []
    for d in range(N_DIR):
        reverse = d == 1
        q_ref, k_ref, v_ref, g_ref = ins[4 * d:4 * d + 4]
        spread_i = (gate_lane == d * N_HEADS + _lane_group(W)).astype(BF16)
        spread_b = (gate_lane == nh + d * N_HEADS + _lane_group(W)).astype(BF16)
        for c in range(cb):
            rows = _chunk_rows(c, cb, reverse)
            blocks.append(dict(d=d, c=c, reverse=reverse, last=0 if reverse else L - 1,
                               causal=(s_pos >= t_pos) if reverse else (s_pos <= t_pos),
                               tri=_tri(L, reverse, False).astype(BF16), spread_i=spread_i, spread_b=spread_b,
                               q=_bf(q_ref[rows, :]), k=k_ref[rows, :], v=_bf(v_ref[rows, :]), g=g_ref[rows, :]))
    for u in blocks:
        u['bc_all'] = _cumsum_mm(u['tri'], u['g'])
    for u in blocks:
        u['src'] = u['g'] - pltpu.roll(u['bc_all'], ZG_W - nh, axis=1)
        u['bb'] = sum(_dot(p, u['spread_b']) for p in _split(u['bc_all'], 3))
        u['li'] = sum(_dot(p, u['spread_i']) for p in _split(u['g'], 3))
    for u in blocks:
        rows8 = sum(_dot_nt(pick, p) for p in _split(u['src'], 3))
        j0 = u['d'] * N_HEADS
        u['src_row'] = jnp.concatenate([rows8[j0 + h:j0 + h + 1, :] for h in range(N_HEADS)], axis=1)
        last = u['last']
        u['b_last'] = u['bb'][last:last + 1, :]
        k4 = jnp.concatenate([_bf(u['k'])] * N_HEADS, axis=0)
        v4 = jnp.concatenate([u['v']] * N_HEADS, axis=0)
        u['k_exp'] = jnp.where(same_head, k4, jnp.zeros_like(k4))
        u['v_exp'] = jnp.where(same_head, v4, jnp.zeros_like(v4))
    for u in blocks:
        u['log_w'] = jnp.where(u['causal'], u['bb'] + u['src_row'], -jnp.inf)
        u['qk'] = _dot_nt(u['q'], u['k_exp'])
        u['gl'] = u['b_last'] - u['bb'] + u['li']
    for u in blocks:
        cols = []
        for half in range(2):
            x = u['log_w'][:, half * 128:(half + 1) * 128]
            lo = jnp.max(jnp.where(lane128 < HEAD_DIM, x, -jnp.inf), axis=1, keepdims=True)
            hi = jnp.max(jnp.where(lane128 >= HEAD_DIM, x, -jnp.inf), axis=1, keepdims=True)
            cols.append(jnp.where(lane128 < HEAD_DIM, lo, hi))
        u['a'] = jnp.concatenate(cols, axis=1)
    for u in blocks:
        last = u['last']
        u['a_last'] = u['a'][last:last + 1, :]
        u['s_loc'] = u['qk'] * jnp.exp(u['log_w'] - u['a'])
    for u in blocks:
        u['wk'] = jnp.exp(u['gl'] - u['a_last']) * u['k']
        u['den_loc'] = _head_sum(u['s_loc'], ones)
        u['num_loc'] = _dot(_bf(u['s_loc']), u['v_exp'])
    for u in blocks:
        u['c_loc'] = jnp.where(same_head, _dot_tn(u['v'], _bf(u['wk'])), 0.0)
        u['n_loc'] = jnp.sum(u['wk'], axis=0, keepdims=True)

    cs = [c_ref[d] for d in range(N_DIR)]
    ns = [n_ref[d:d + 1, :] for d in range(N_DIR)]
    ms = [m_ref[d:d + 1, :] for d in range(N_DIR)]
    for c in range(cb):
        now = [u for u in blocks if u['c'] == c]
        for u in now:
            d = u['d']
            u['c_in'], u['n_in'], u['m_in'] = cs[d], ns[d], ms[d]
            u['m_new'] = jnp.maximum(u['b_last'] + ms[d], u['a_last'])
        for u in now:
            u['carry'] = jnp.exp(u['b_last'] + u['m_in'] - u['m_new'])
            u['fresh'] = jnp.exp(u['a_last'] - u['m_new'])
        for u in now:
            d = u['d']
            cs[d] = u['carry'] * cs[d] + u['fresh'] * u['c_loc']
            ns[d] = u['carry'] * ns[d] + u['fresh'] * u['n_loc']
            ms[d] = u['m_new']

    for u in blocks:
        u['qc'] = _dot_nt(u['q'], _bf(u['c_in']))
        u['qn'] = _head_sum(u['q'].astype(F32) * u['n_in'], ones)
        u['m_t'] = jnp.maximum(u['bb'] + u['m_in'], u['a'])
    for u in blocks:
        u['scale'] = jnp.exp(u['a'] - u['m_t'])
        u['inter'] = jnp.exp(u['bb'] + u['m_in'] - u['m_t'])
        u['floor'] = jnp.exp(-u['m_t'])
    for u in blocks:
        u['num'] = u['scale'] * u['num_loc'] + u['inter'] * u['qc']
        u['den'] = jnp.maximum(jnp.abs(u['scale'] * u['den_loc'] + u['inter'] * u['qn']), u['floor'])
    for u in blocks:
        h_ref = h1_ref if u['reverse'] else h0_ref
        h_ref[_chunk_rows(u['c'], cb, u['reverse']), :] = u['num'] / u['den']
    for d in range(N_DIR):
        c_ref[d] = cs[d]
        n_ref[d:d + 1, :] = ns[d]
        m_ref[d:d + 1, :] = ms[d]

    @pl.when(i == pl.num_programs(1) - 1)
    def _():
        for d in range(N_DIR):
            for h in range(N_HEADS):
                hs = slice(h * HEAD_DIM, (h + 1) * HEAD_DIM)
                cfin_ref[0, d, h] = c_ref[d, hs, hs]
        nfin_ref[0] = n_ref[...]
        mfin_ref[0] = m_ref[...]


def mlstm_scan(q, k, z, g, init, bsz, t):
    cb = SCAN_CB
    blk = cb * CHUNK
    assert t % blk == 0
    nc = t // blk
    vcol = (ZD_OFF + 2 * D_GROUP) // D_GROUP

    def specs(row):
        return [pl.BlockSpec((blk, D_GROUP), lambda bi, i: (row(bi, i), 0)),
                pl.BlockSpec((blk, D_GROUP), lambda bi, i: (row(bi, i), 0)),
                pl.BlockSpec((blk, D_GROUP), lambda bi, i: (row(bi, i), vcol)),
                pl.BlockSpec((blk, ZG_W), lambda bi, i: (row(bi, i), 0))]

    fwd = specs(lambda bi, i: bi * nc + i)
    bwd = specs(lambda bi, i: bi * nc + nc - 1 - i)
    cst = pl.BlockSpec((1, N_DIR, N_HEADS, HEAD_DIM, HEAD_DIM), lambda bi, i: (bi, 0, 0, 0, 0))
    vst = pl.BlockSpec((1, N_DIR, D_GROUP), lambda bi, i: (bi, 0, 0))
    has_init = init is not None
    args = (list(init) if has_init else []) + [q, k, z, g] * 2
    tok = jax.ShapeDtypeStruct((bsz * t, D_GROUP), F32)
    vec = jax.ShapeDtypeStruct((bsz, N_DIR, D_GROUP), F32)
    return pl.pallas_call(
        functools.partial(_mlstm_scan_kernel, has_init, cb),
        grid=(bsz, nc),
        in_specs=([cst, vst, vst] if has_init else []) + fwd + bwd,
        out_specs=[fwd[0], bwd[0], cst, vst, vst],
        out_shape=[tok, tok, jax.ShapeDtypeStruct((bsz, N_DIR, N_HEADS, HEAD_DIM, HEAD_DIM), F32), vec, vec],
        scratch_shapes=[pltpu.VMEM((N_DIR, D_GROUP, D_GROUP), F32),
                        pltpu.VMEM((N_DIR, D_GROUP), F32),
                        pltpu.VMEM((N_DIR, D_GROUP), F32)],
        compiler_params=_params("parallel", "arbitrary"),
        name="mlstm_scan",
    )(*args)


def _mix_out(x_ref, mod_ref, y0_ref, y1_ref, bon_ref, g_ref, lng_ref, lnb_ref, pd_ref, pw_ref, ps_ref,
             cv_ref, clg_ref, clb_ref, cpw_ref, h0_ref, h1_ref, o_ref, hng_ref, w_ref):
    ones = _head_ones()
    inv = 1.0 / HEAD_DIM
    y = y0_ref[...] + y1_ref[...] + bon_ref[...]
    mu = _head_sum(y, ones) * inv
    yc = y - mu
    var = _head_sum(yc * yc, ones) * inv
    ya = (yc * lax.rsqrt(var + RWKV_LN_EPS) * lng_ref[...] + lnb_ref[...]) * g_ref[...]
    y_pool = _mm(pd_ref[...], pw_ref[...]) * ps_ref[...]
    cv = cv_ref[...]
    cmu = jnp.mean(cv, axis=-1, keepdims=True)
    cvar = jnp.mean(jnp.square(cv - cmu), axis=-1, keepdims=True)
    cn = (cv - cmu) * lax.rsqrt(cvar + CONV_LN_EPS) * clg_ref[...] + clb_ref[...]
    y_conv = _mm(cn * _sigmoid(cn), cpw_ref[...])
    hs = (h0_ref[...] + h1_ref[...]) * _sigmoid(o_ref[...])
    yd = hs * lax.rsqrt(_head_sum(hs * hs, ones) * inv + EPS) * hng_ref[...]
    g = D_GROUP
    mixed = (_mm(ya, w_ref[0:g, :]) + _mm(y_pool, w_ref[g:2 * g, :])
             + _mm(y_conv, w_ref[2 * g:3 * g, :]) + _mm(yd, w_ref[3 * g:4 * g, :]))
    return x_ref[...] + mod_ref[0, 2:3, :] * mixed


def _mlp(final, x, mod_ref, g_ref, w1_ref, b1_ref, w2_ref, b2_ref, fg_ref):
    h = _rms(x, g_ref[...]) * (1.0 + mod_ref[0, 4:5, :]) + mod_ref[0, 3:4, :]
    a = jnp.maximum(jnp.dot(h.astype(BF16), w1_ref[...], preferred_element_type=F32) + b1_ref[...], 0.0)
    f = jnp.dot((a * a).astype(BF16), w2_ref[...], preferred_element_type=F32) + b2_ref[...]
    x = x + mod_ref[0, 5:6, :] * f
    return _rms(x, fg_ref[...]) if final else x


def _mix_mlp_kernel(final, *refs):
    mix_refs, mlp_refs, out_ref = refs[:20], refs[20:26], refs[26]
    x = _mix_out(*mix_refs)
    out_ref[...] = _mlp(final, x, mix_refs[1], *mlp_refs)


def mix_mlp(x, mod, ra, pooled, conv, md, z, p, final_g, t, final):
    n = x.shape[0]
    tok = pl.BlockSpec((ROW_TILE, D_GROUP), lambda i: (i, 0))
    ocol = (ZD_OFF + 3 * D_GROUP) // D_GROUP
    vec = _const((1, D_GROUP))
    wide = _const((1, D_MODEL))
    sq = _const((D_GROUP, D_GROUP))
    row = pl.BlockSpec((ROW_TILE, D_MODEL), lambda i: (i, 0))
    return pl.pallas_call(
        functools.partial(_mix_mlp_kernel, final),
        grid=(n // ROW_TILE,),
        in_specs=[row, _mod_spec(mod.shape[0] > 1, t), tok, tok, tok, tok, vec, vec, tok, sq, vec,
                  tok, vec, vec, sq, tok, tok,
                  pl.BlockSpec((ROW_TILE, D_GROUP), lambda i: (i, ocol)), vec, _const((D_MODEL, D_MODEL)),
                  wide, _const((D_MODEL, D_FF)), _const((1, D_FF)), _const((D_FF, D_MODEL)), wide, wide],
        out_specs=row,
        out_shape=jax.ShapeDtypeStruct((n, D_MODEL), F32),
        compiler_params=_params("parallel"),
        name="mix_mlp",
    )(x, mod, ra['y0'], ra['y1'], ra['bon'], ra['g'], p['rwkv_ln_g'], p['rwkv_ln_b'],
      pooled, p['pool_w'], p['pool_scale'], conv, p['conv']['ln_g'], p['conv']['ln_b'], p['conv']['pw'],
      md['h0'], md['h1'], z, p['hn_g'], p['w_out'],
      p['norm2_g'], p['w1'], p['b1'], p['w2'], p['b2'], final_g)


def _trunk_layer(x, mod, p, states, bsz, t, grid, final_g, final):
    z = in_proj(x, mod, p['norm1_g'], p['w_in'], t)
    r, kk, v, g, bon, lw0, lw1, kd0, kd1, b0, b1 = rwkv_prep(z, p['rwkv'], t)
    y0, y1, s_rwkv = rwkv_scan(r, kk, v, (lw0, lw1), (kd0, kd1), (b0, b1),
                               None if states is None else states[0], bsz, t)
    yb = pool_mix(z, bsz, t, grid)
    yc = conv_mix(z, p['conv'], bsz, t)
    q, k, gates = mlstm_prep(z, p['mlstm'], t)
    h0, h1, s_c, s_n, s_m = mlstm_scan(q, k, z, gates, None if states is None else states[1:], bsz, t)
    x = mix_mlp(x, mod, dict(y0=y0, y1=y1, bon=bon, g=g), yb, yc, dict(h0=h0, h1=h1), z, p, final_g, t, final)
    return x, (s_rwkv, s_c, s_n, s_m)


def _layer_params(l, norm1_g, norm2_g, w_in, w_out, rwkv_mu, rwkv_w0, rwkv_w_up, rwkv_a0, rwkv_a_up, rwkv_g_up,
                  rwkv_k_k, rwkv_k_a, rwkv_r_k, rwkv_ln_g, rwkv_ln_b, pool_w, pool_scale, conv_dw, conv_b,
                  conv_ln_g, conv_ln_b, conv_pw, mlstm_qk_conv, mlstm_i_bias, mlstm_f_bias, mlstm_hn_g,
                  mlp_w1, mlp_b1, mlp_w2, mlp_b2):
    row = lambda a: a[l].reshape(1, -1)
    w = w_in[l]
    a_w, b_w, c_w = 1024, 256, 512
    wa, wb, wc, wd = w[:, :a_w], w[:, a_w:a_w + b_w], w[:, a_w + b_w:a_w + b_w + c_w], w[:, a_w + b_w + c_w:]
    nh = N_DIR * N_HEADS
    qkv, gates, o = wd[:, :3 * D_GROUP], wd[:, 3 * D_GROUP:3 * D_GROUP + 2 * nh], wd[:, 3 * D_GROUP + 2 * nh:]
    w_cat = jnp.concatenate([wa, qkv, o, wc, wb, gates, jnp.zeros((D_MODEL, ZG_W - 2 * nh), F32)], axis=1)
    rank = rwkv_w_up.shape[2]
    zeros = jnp.zeros((N_DIR, rank, D_GROUP), F32)
    pool_bd = jax.scipy.linalg.block_diag(*[pool_w[l, g] for g in range(len(POOL_HALF))])
    gate_bias = jnp.concatenate([mlstm_i_bias[l], mlstm_f_bias[l], jnp.zeros((ZG_W - 2 * nh,), F32)]).reshape(1, ZG_W)
    return dict(
        norm1_g=row(norm1_g), norm2_g=row(norm2_g), w_in=w_cat.astype(BF16), w_out=w_out[l].astype(BF16),
        rwkv=dict(mu=row(rwkv_mu), k_k=row(rwkv_k_k), k_a=row(rwkv_k_a), r_k=row(rwkv_r_k),
                  w0=rwkv_w0[l].reshape(N_DIR, 1, D_GROUP), a0=rwkv_a0[l].reshape(N_DIR, 1, D_GROUP),
                  w_up=jnp.concatenate([rwkv_w_up[l], zeros], axis=1).astype(BF16),
                  a_up=jnp.concatenate([zeros, rwkv_a_up[l]], axis=1).astype(BF16),
                  g_up=rwkv_g_up[l].astype(BF16)),
        rwkv_ln_g=row(rwkv_ln_g), rwkv_ln_b=row(rwkv_ln_b),
        pool_w=pool_bd.astype(BF16), pool_scale=row(pool_scale),
        conv=dict(dw=conv_dw[l], b=row(conv_b), ln_g=row(conv_ln_g), ln_b=row(conv_ln_b), pw=conv_pw[l].astype(BF16)),
        mlstm=dict(qk_conv=mlstm_qk_conv[l], gate_bias=gate_bias),
        hn_g=row(mlstm_hn_g),
        w1=mlp_w1[l].astype(BF16), b1=row(mlp_b1), w2=mlp_w2[l].astype(BF16), b2=row(mlp_b2),
    )


def kernel(x_prompt, x_sample, c, state_rwkv, state_mlstm_C, state_mlstm_n, state_mlstm_m, c_ctx, w_mod, b_mod, norm1_g, norm2_g, w_in, w_out, rwkv_mu, rwkv_w0, rwkv_w_up, rwkv_a0, rwkv_a_up, rwkv_g_up, rwkv_k_k, rwkv_k_a, rwkv_r_k, rwkv_ln_g, rwkv_ln_b, pool_w, pool_scale, conv_dw, conv_b, conv_ln_g, conv_ln_b, conv_pw, mlstm_qk_conv, mlstm_i_bias, mlstm_f_bias, mlstm_hn_g, mlp_w1, mlp_b1, mlp_w2, mlp_b2, final_g):
    bp, tp, d = x_prompt.shape
    bs, ts, _ = x_sample.shape
    nh = N_DIR * N_HEADS
    cvec = jnp.concatenate([c_ctx[None, :], c, jnp.zeros((SUB - 1 - bs, d), F32)], axis=0)
    mods = modulation(cvec, w_mod, b_mod).reshape(DEPTH, SUB, 6, d)
    xp = x_prompt.reshape(bp * tp, d)
    xs = x_sample.reshape(bs * ts, d)
    fg = final_g.reshape(1, d)
    new = [[], [], [], []]
    for l in range(DEPTH):
        p = _layer_params(l, norm1_g, norm2_g, w_in, w_out, rwkv_mu, rwkv_w0, rwkv_w_up, rwkv_a0, rwkv_a_up,
                          rwkv_g_up, rwkv_k_k, rwkv_k_a, rwkv_r_k, rwkv_ln_g, rwkv_ln_b, pool_w, pool_scale,
                          conv_dw, conv_b, conv_ln_g, conv_ln_b, conv_pw, mlstm_qk_conv, mlstm_i_bias,
                          mlstm_f_bias, mlstm_hn_g, mlp_w1, mlp_b1, mlp_w2, mlp_b2)
        final = l == DEPTH - 1
        xp, st = _trunk_layer(xp, mods[l, 0:1], p, None, bp, tp, False, fg, final)
        for acc, s in zip(new, st):
            acc.append(s)
        cached = (state_rwkv[:, l], state_mlstm_C[:, l],
                  state_mlstm_n[:, l].reshape(bs, N_DIR, D_GROUP),
                  jnp.repeat(state_mlstm_m[:, l], HEAD_DIM, axis=-1))
        xs, _ = _trunk_layer(xs, mods[l, 1:1 + bs], p, cached, bs, ts, True, fg, final)
    return (xp.reshape(bp, tp, d), xs.reshape(bs, ts, d),
            jnp.stack(new[0], axis=1), jnp.stack(new[1], axis=1),
            jnp.stack(new[2], axis=1).reshape(bp, DEPTH, N_DIR, N_HEADS, HEAD_DIM),
            jnp.stack(new[3], axis=1).reshape(bp, DEPTH, N_DIR, N_HEADS, HEAD_DIM)[..., 0])
```

```python
import functools
import math

import jax
import jax.numpy as jnp
from jax import lax
from jax.experimental import pallas as pl
from jax.experimental.pallas import tpu as pltpu

F32 = jnp.float32
BF16 = jnp.bfloat16

D_MODEL = 1024
D_GROUP = 256
HEAD_DIM = 64
N_HEADS = 4
N_DIR = 2
DEPTH = 4
GRID_W = 64
POOL_HALF = (1, 2, 4, 8)
MAX_HALF = POOL_HALF[-1]
CONV_W = 31
CONV_PAD = 16
D_FF = 4 * D_MODEL
EPS = 1e-6
RWKV_LN_EPS = 64e-5
CONV_LN_EPS = 1e-5
CHUNK = 64
ROW_TILE = 512
SUB = 8
VMEM_LIMIT = 56 * 1024 * 1024

Z_W = 2944
ZA_W, ZD_QK_W, ZC_W, ZB_W, ZG_W = 1024, 512, 512, 256, 128
ZD_OFF, ZC_OFF, ZB_OFF, ZG_OFF = 1024, 2048, 2560, 2816
ZREST_OFF, ZREST_W = 1536, 1280


def _mm(a, b):
    return jnp.dot(a.astype(BF16), b.astype(BF16), preferred_element_type=F32)


def _split(x, parts):
    out = []
    for _ in range(parts):
        p = x.astype(BF16)
        out.append(p)
        x = x - p.astype(F32)
    return out


def _head_sum(x, ones):
    return sum(jnp.dot(p, ones, preferred_element_type=F32) for p in _split(x, 2))


def _cumsum_mm(tri, x):
    return sum(jnp.dot(tri, p, preferred_element_type=F32) for p in _split(x, 3))


def _tri(n, reverse, strict):
    t = lax.broadcasted_iota(jnp.int32, (n, n), 0)
    s = lax.broadcasted_iota(jnp.int32, (n, n), 1)
    if reverse:
        return (s > t) if strict else (s >= t)
    return (s < t) if strict else (s <= t)


def _head_ones():
    a = lax.broadcasted_iota(jnp.int32, (D_GROUP, D_GROUP), 0) // HEAD_DIM
    b = lax.broadcasted_iota(jnp.int32, (D_GROUP, D_GROUP), 1) // HEAD_DIM
    return (a == b).astype(BF16)


def _sigmoid(x):
    return jax.nn.sigmoid(x)


def _softplus(x):
    return jnp.maximum(x, 0.0) + jnp.log1p(jnp.exp(-jnp.abs(x)))


def _rms(x, g):
    return x * lax.rsqrt(jnp.mean(x * x, axis=-1, keepdims=True) + EPS) * g


def _params(*sem):
    return pltpu.CompilerParams(dimension_semantics=sem, vmem_limit_bytes=VMEM_LIMIT)


def _const(shape):
    return pl.BlockSpec(shape, lambda *_: (0,) * len(shape), pipeline_mode=pl.Buffered(1))


def _halo_specs(width, col, n_rows, tt):
    per = tt // SUB
    last = n_rows // SUB - 1
    prev = pl.BlockSpec((SUB, width), lambda i: (jnp.maximum(i * per - 1, 0), col))
    nxt = pl.BlockSpec((SUB, width), lambda i: (jnp.minimum((i + 1) * per, last), col))
    return prev, nxt


def _mod_kernel(c_ref, w_ref, b_ref, o_ref):
    c = c_ref[...]
    o_ref[0] = _mm(c * _sigmoid(c), w_ref[0]) + b_ref[0]


def modulation(cvec, w_mod, b_mod):
    tn = 1536
    n = w_mod.shape[-1]
    return pl.pallas_call(
        _mod_kernel,
        grid=(DEPTH, n // tn),
        in_specs=[pl.BlockSpec((SUB, D_MODEL), lambda l, j: (0, 0)),
                  pl.BlockSpec((1, D_MODEL, tn), lambda l, j: (l, 0, j)),
                  pl.BlockSpec((1, 1, tn), lambda l, j: (l, 0, j))],
        out_specs=pl.BlockSpec((1, SUB, tn), lambda l, j: (l, 0, j)),
        out_shape=jax.ShapeDtypeStruct((DEPTH, SUB, n), F32),
        compiler_params=_params("parallel", "parallel"),
        name="modulation",
    )(cvec, w_mod, b_mod.reshape(DEPTH, 1, n))


def _mod_spec(per_batch, rows_per_seq):
    if per_batch:
        return pl.BlockSpec((1, 6, D_MODEL), lambda i: (i * ROW_TILE // rows_per_seq, 0, 0))
    return pl.BlockSpec((1, 6, D_MODEL), lambda i: (0, 0, 0))


def _shifted(ze, i, t):
    prev = ze[SUB - 1:SUB - 1 + ROW_TILE, :]
    nxt = ze[SUB + 1:SUB + 1 + ROW_TILE, :]
    pos = (i * ROW_TILE + lax.broadcasted_iota(jnp.int32, (ROW_TILE, 1), 0)) % t
    return jnp.where(pos == 0, 0.0, prev), jnp.where(pos == t - 1, 0.0, nxt)


def _in_proj_kernel(t, x_ref, xp_ref, xn_ref, mod_ref, g_ref, w_ref,
                    mu_ref, kk_ref, ka_ref, rk_ref, w0_ref, a0_ref, wup_ref, aup_ref, gup_ref, cw_ref, gb_ref,
                    z_o, r_o, kk_o, v_o, g_o, bon_o, lw0_o, lw1_o, kd0_o, kd1_o, b0_o, b1_o, q_o, k_o, gate_o):
    i = pl.program_id(0)
    xe = jnp.concatenate([xp_ref[...], x_ref[...], xn_ref[...]], axis=0)
    h = _rms(xe, g_ref[...]) * (1.0 + mod_ref[0, 1:2, :]) + mod_ref[0, 0:1, :]
    ze = jnp.dot(h.astype(BF16), w_ref[...], preferred_element_type=F32)
    mid = slice(SUB, SUB + ROW_TILE)
    z_o[...] = ze[mid, ZREST_OFF:ZREST_OFF + ZREST_W]

    za = ze[:, 0:ZA_W]
    z = za[mid, :]
    prev, nxt = _shifted(za, i, t)
    zm = z + (0.5 * (prev + nxt) - z) * mu_ref[...]
    r, k, v = zm[:, 0:256], zm[:, 256:512], zm[:, 512:768]
    lora = zm[:, 768:896]
    gd = zm[:, 896:1024]
    ones = _head_ones()
    kk = k * kk_ref[...]
    kk = kk * lax.rsqrt(_head_sum(kk * kk, ones) + 1e-12)
    wt = jnp.tanh(lora)
    r_o[...] = r
    kk_o[...] = kk
    v_o[...] = v
    g_o[...] = _mm(_sigmoid(gd), gup_ref[...])
    bon = jnp.zeros_like(r)
    for d, (lw_o, kd_o, b_o) in enumerate(((lw0_o, kd0_o, b0_o), (lw1_o, kd1_o, b1_o))):
        lw_o[...] = -math.exp(-0.5) * _sigmoid(w0_ref[d] + _mm(wt, wup_ref[d]))
        a = _sigmoid(a0_ref[d] + _mm(lora, aup_ref[d]))
        kd = k * (1.0 + (a - 1.0) * ka_ref[...])
        kd_o[...] = kd
        b_o[...] = -a * kk
        bon = bon + _head_sum(r * kd * rk_ref[...], ones) * v
    bon_o[...] = bon

    zq = ze[:, ZD_OFF:ZD_OFF + ZD_QK_W]
    prev, nxt = _shifted(zq, i, t)
    qk = cw_ref[0:1, :] * prev + cw_ref[1:2, :] * zq[mid, :] + cw_ref[2:3, :] * nxt
    qk = qk * _sigmoid(qk)
    q_o[...] = qk[:, :D_GROUP]
    k_o[...] = qk[:, D_GROUP:] * (1.0 / math.sqrt(HEAD_DIM))
    gb = ze[mid, ZG_OFF:ZG_OFF + ZG_W] + gb_ref[...]
    lane = lax.broadcasted_iota(jnp.int32, gb.shape, 1)
    gate_o[...] = jnp.where(lane < N_DIR * N_HEADS, gb, -_softplus(-gb))


def in_proj(x, mod, p, t):
    n = x.shape[0]
    tile = pl.BlockSpec((ROW_TILE, D_MODEL), lambda i: (i, 0))
    prev, nxt = _halo_specs(D_MODEL, 0, n, ROW_TILE)
    out = pl.BlockSpec((ROW_TILE, D_GROUP), lambda i: (i, 0))
    vec = _const((1, D_GROUP))
    rw, ml = p['rwkv'], p['mlstm']
    tok = jax.ShapeDtypeStruct((n, D_GROUP), F32)
    return pl.pallas_call(
        functools.partial(_in_proj_kernel, t),
        grid=(n // ROW_TILE,),
        in_specs=[tile, prev, nxt, _mod_spec(mod.shape[0] > 1, t), _const((1, D_MODEL)), _const((D_MODEL, Z_W)),
                  _const((1, ZA_W)), vec, vec, vec,
                  _const((N_DIR, 1, D_GROUP)), _const((N_DIR, 1, D_GROUP)),
                  _const((N_DIR, 128, D_GROUP)), _const((N_DIR, 128, D_GROUP)), _const((128, D_GROUP)),
                  _const((3, ZD_QK_W)), _const((1, ZG_W))],
        out_specs=[pl.BlockSpec((ROW_TILE, ZREST_W), lambda i: (i, 0))] + [out] * 13
                  + [pl.BlockSpec((ROW_TILE, ZG_W), lambda i: (i, 0))],
        out_shape=[jax.ShapeDtypeStruct((n, ZREST_W), F32)] + [tok] * 13 + [jax.ShapeDtypeStruct((n, ZG_W), F32)],
        compiler_params=_params("parallel"),
        name="in_proj",
    )(x, x, x, mod, p['norm1_g'], p['w_in'], rw['mu'], rw['k_k'], rw['k_a'], rw['r_k'], rw['w0'], rw['a0'],
      rw['w_up'], rw['a_up'], rw['g_up'], ml['qk_conv'], ml['gate_bias'])


SCAN_CB = 4


def _bf(x):
    return x.astype(BF16)


def _dot(a, b):
    return jnp.dot(a, b, preferred_element_type=F32)


def _dot_nt(a, b):
    return lax.dot_general(a, b, (((1,), (1,)), ((), ())), preferred_element_type=F32)


def _dot_tn(a, b):
    return lax.dot_general(a, b, (((0,), (0,)), ((), ())), preferred_element_type=F32)


def _tri2(reverse, strict):
    t = lax.broadcasted_iota(jnp.int32, (CHUNK, 2 * CHUNK), 0)
    s = lax.broadcasted_iota(jnp.int32, (CHUNK, 2 * CHUNK), 1) % CHUNK
    if reverse:
        return (s > t) if strict else (s >= t)
    return (s < t) if strict else (s <= t)


def _chunk_rows(c, cb, reverse):
    return slice((cb - 1 - c) * CHUNK, (cb - c) * CHUNK) if reverse else slice(c * CHUNK, (c + 1) * CHUNK)


def _rwkv_scan_kernel(has_init, cb, *refs):
    if has_init:
        s0_ref, refs = refs[0], refs[1:]
    ins, (y0_ref, y1_ref, sfin_ref, s_ref) = refs[:12], refs[12:]
    L = CHUNK
    i = pl.program_id(1)

    @pl.when(i == 0)
    def _():
        if has_init:
            s_ref[...] = s0_ref[0]
        else:
            s_ref[...] = jnp.zeros_like(s_ref)

    eye = (lax.broadcasted_iota(jnp.int32, (L, L), 0) == lax.broadcasted_iota(jnp.int32, (L, L), 1)).astype(F32)
    heads = [slice(h * HEAD_DIM, (h + 1) * HEAD_DIM) for h in range(N_HEADS)]

    blocks = []
    for d in range(N_DIR):
        reverse = d == 1
        incl = _tri(L, reverse, False)
        for c in range(cb):
            rows = _chunk_rows(c, cb, reverse)
            blocks.append(dict(d=d, c=c, rows=rows, reverse=reverse, incl=incl, lw=ins[6 * d + 3][rows, :]))
    for u in blocks:
        u['cum'] = _cumsum_mm(u['incl'].astype(BF16), u['lw'])
    inst = []
    for u in blocks:
        d, rows, cum, lw = u['d'], u['rows'], u['cum'], u['lw']
        r_ref, kk_ref, v_ref, _, kd_ref, b_ref = ins[6 * d:6 * d + 6]
        r, kk, v, kd, b = (ref[rows, :] for ref in (r_ref, kk_ref, v_ref, kd_ref, b_ref))
        last = cum[0:1, :] if u['reverse'] else cum[L - 1:L, :]
        e_neg = jnp.exp(-cum)
        e_end = jnp.exp(last - cum)
        e_all = jnp.exp(last)
        aq = _bf(kk * jnp.exp(cum - lw))
        rq = r * jnp.exp(cum)
        bk = _bf(b * e_neg)
        kq = _bf(kd * e_neg)
        k_end = _bf(kd * e_end)
        b_end = _bf(b * e_end)
        rq_b, v_b = _bf(rq), _bf(v)
        for h, hs in enumerate(heads):
            inst.append(dict(d=d, c=u['c'], h=h, reverse=u['reverse'], aq=aq[:, hs],
                             rq=rq[:, hs], rq_b=rq_b[:, hs], bk=bk[:, hs], kq=kq[:, hs], k_end=k_end[:, hs],
                             b_end=b_end[:, hs], v=v_b[:, hs], e_all=e_all[:, hs]))

    zeros_h = jnp.zeros((L, HEAD_DIM), BF16)
    for t in inst:
        res = _dot_nt(jnp.concatenate([t['aq'], t['rq_b']], axis=0), jnp.concatenate([t['bk'], t['kq']], axis=0))
        t['mn'] = jnp.where(_tri2(t['reverse'], True), res[:L], 0.0)
        t['pp'] = _bf(jnp.where(_tri2(t['reverse'], False), res[L:], 0.0))
        t['m'] = t['mn'][:, :L]
    row = lax.broadcasted_iota(jnp.int32, (L, L), 0)
    col = lax.broadcasted_iota(jnp.int32, (L, L), 1)
    sizes = [2 ** e for e in range(int(math.log2(L)))]
    corner = {s: jnp.logical_and(row // (2 * s) == col // (2 * s), row // s != col // s) for s in sizes}
    for t in inst:
        t['y'] = eye + jnp.where(corner[1], t['m'], 0.0)
    for s in sizes[1:]:
        for t in inst:
            t['my'] = _bf(_dot(_bf(jnp.where(corner[s], t['m'], 0.0)), _bf(t['y'])))
        for t in inst:
            t['y'] = t['y'] + _dot(_bf(t['y']), t['my'])
    for t in inst:
        t['y'] = _bf(t['y'])
        t['nv'] = _bf(_dot(_bf(t['mn']), jnp.concatenate([zeros_h, t['v']], axis=0)))
    for t in inst:
        t['w'] = _bf(_dot(t['y'], t['nv']))
        t['q'] = _bf(_dot(t['y'], t['aq']))
    for t in inst:
        t['y_loc'] = _dot(t['pp'], jnp.concatenate([t['w'], t['v']], axis=0))
        t['r2'] = _bf(t['rq'] + _dot(t['pp'], jnp.concatenate([t['q'], zeros_h], axis=0)))
        t['tq'] = _bf(_dot_tn(t['q'], t['b_end']))
        t['g'] = _dot_tn(jnp.concatenate([t['v'], t['w']], axis=0),
                         jnp.concatenate([t['k_end'], t['b_end']], axis=0))

    state = [[s_ref[d, h] for h in range(N_HEADS)] for d in range(N_DIR)]
    outs = {}
    for c in range(cb):
        now = [t for t in inst if t['c'] == c]
        for t in now:
            s0 = state[t['d']][t['h']]
            s0_b = _bf(s0)
            outs[(t['d'], c, t['h'])] = t['y_loc'] + _dot_nt(t['r2'], s0_b)
            t['s1'] = s0 * t['e_all'] + _dot(s0_b, t['tq']) + t['g']
        for t in now:
            state[t['d']][t['h']] = t['s1']
    for d, y_ref in enumerate((y0_ref, y1_ref)):
        for c in range(cb):
            y_ref[_chunk_rows(c, cb, d == 1), :] = jnp.concatenate([outs[(d, c, h)] for h in range(N_HEADS)], axis=1)
    for d in range(N_DIR):
        for h in range(N_HEADS):
            s_ref[d, h] = state[d][h]

    @pl.when(i == pl.num_programs(1) - 1)
    def _():
        sfin_ref[0] = s_ref[...]


def rwkv_scan(r, kk, v, lw, kd, b, s0, bsz, t):
    cb = SCAN_CB
    blk = cb * CHUNK
    assert t % blk == 0
    nc = t // blk
    fwd = pl.BlockSpec((blk, D_GROUP), lambda bi, i: (bi * nc + i, 0))
    bwd = pl.BlockSpec((blk, D_GROUP), lambda bi, i: (bi * nc + nc - 1 - i, 0))
    st = pl.BlockSpec((1, N_DIR, N_HEADS, HEAD_DIM, HEAD_DIM), lambda bi, i: (bi, 0, 0, 0, 0))
    has_init = s0 is not None
    args = ([s0] if has_init else []) + [r, kk, v, lw[0], kd[0], b[0], r, kk, v, lw[1], kd[1], b[1]]
    in_specs = ([st] if has_init else []) + [fwd] * 6 + [bwd] * 6
    tok = jax.ShapeDtypeStruct((bsz * t, D_GROUP), F32)
    return pl.pallas_call(
        functools.partial(_rwkv_scan_kernel, has_init, cb),
        grid=(bsz, nc),
        in_specs=in_specs,
        out_specs=[fwd, bwd, st],
        out_shape=[tok, tok, jax.ShapeDtypeStruct((bsz, N_DIR, N_HEADS, HEAD_DIM, HEAD_DIM), F32)],
        scratch_shapes=[pltpu.VMEM((N_DIR, N_HEADS, HEAD_DIM, HEAD_DIM), F32)],
        compiler_params=_params("parallel", "arbitrary"),
        name="rwkv_scan",
    )(*args)


def _shift_rows(x, k, pos, n):
    rows = x.shape[0]
    rolled = pltpu.roll(x, k % rows, axis=0)
    ok = (pos >= k) if k > 0 else (pos < n + k)
    return jnp.where(ok, rolled, 0.0)


def _window_sums(x, pos, n):
    fw = x
    bw = _shift_rows(x, 1, pos, n)
    out = [fw + bw]
    for h in POOL_HALF[:-1]:
        fw = fw + _shift_rows(fw, -h, pos, n)
        bw = bw + _shift_rows(bw, h, pos, n)
        out.append(fw + bw)
    return out


def _by_group(vals, lane):
    out = vals[-1]
    for g in range(len(vals) - 2, -1, -1):
        out = jnp.where(lane < (g + 1) * HEAD_DIM, vals[g], out)
    return out


def _clipped_count(pos, h, n):
    return jnp.minimum(pos + h, n) - jnp.maximum(pos - h, 0)


def _pool_seq_kernel(x_ref, o_ref):
    x = x_ref[...]
    t = x.shape[0]
    pos = lax.broadcasted_iota(jnp.int32, x.shape, 0)
    lane = lax.broadcasted_iota(jnp.int32, x.shape, 1)
    sums = _by_group(_window_sums(x, pos, t), lane)
    cnt = _by_group([_clipped_count(pos, h, t) for h in POOL_HALF], lane).astype(F32)
    o_ref[...] = sums / cnt - x


def _pool_grid_kernel(x_ref, o_ref, cs_ref):
    gw = GRID_W
    rows = x_ref.shape[0] // gw
    shape = (gw, D_GROUP)
    col = lax.broadcasted_iota(jnp.int32, shape, 0)
    lane = lax.broadcasted_iota(jnp.int32, shape, 1)
    half = _by_group([jnp.full(shape, h, jnp.int32) for h in POOL_HALF], lane)
    cnt_c = _by_group([_clipped_count(col, h, gw) for h in POOL_HALF], lane)

    def col_stage(r, _):
        at = pl.ds(pl.multiple_of(r * gw, gw), gw)
        cs_ref[at, :] = _by_group(_window_sums(x_ref[at, :], col, gw), lane)
        return 0

    lax.fori_loop(0, rows, col_stage, 0)

    def row_stage(r, _):
        at = pl.ds(pl.multiple_of(r * gw, gw), gw)
        acc = jnp.zeros(shape, F32)
        for o in range(-MAX_HALF, MAX_HALF):
            rr = r + o
            src = pl.ds(pl.multiple_of(jnp.clip(rr, 0, rows - 1) * gw, gw), gw)
            in_window = (half > o) if o >= 0 else (half >= -o)
            ok = jnp.logical_and(in_window, jnp.logical_and(rr >= 0, rr < rows))
            acc = acc + jnp.where(ok, cs_ref[src, :], 0.0)
        cnt = (_clipped_count(r, half, rows) * cnt_c).astype(F32)
        o_ref[at, :] = acc / cnt - x_ref[at, :]
        return 0

    lax.fori_loop(0, rows, row_stage, 0)


def pool_mix(z, bsz, t, grid):
    n = bsz * t
    col = (ZB_OFF - ZREST_OFF) // ZB_W
    scratch = [pltpu.VMEM((t, D_GROUP), F32)] if grid else []
    return pl.pallas_call(
        _pool_grid_kernel if grid else _pool_seq_kernel,
        grid=(bsz,),
        in_specs=[pl.BlockSpec((t, ZB_W), lambda i: (i, col))],
        out_specs=pl.BlockSpec((t, D_GROUP), lambda i: (i, 0)),
        out_shape=jax.ShapeDtypeStruct((n, D_GROUP), F32),
        scratch_shapes=scratch,
        compiler_params=_params("parallel"),
        name="pool_grid" if grid else "pool_seq",
    )(z)


CONV_TILE = 64


def _conv_kernel(z_ref, dw_ref, b_ref, o_ref, pad_ref):
    t = z_ref.shape[0]
    rt = CONV_TILE
    edge = jnp.zeros((CONV_PAD, D_GROUP), F32)
    pad_ref[0:CONV_PAD, :] = edge
    pad_ref[t + CONV_PAD:t + 2 * CONV_PAD, :] = edge

    def glu(i, _):
        base = pl.multiple_of(i * rt, rt)
        zz = z_ref[pl.ds(base, rt), :]
        pad_ref[pl.ds(base + CONV_PAD, rt), :] = zz[:, :D_GROUP] * _sigmoid(zz[:, D_GROUP:])
        return 0

    lax.fori_loop(0, t // rt, glu, 0)

    def tile(i, _):
        base = pl.multiple_of(i * rt, rt)
        rows = rt + 2 * CONV_PAD
        first = CONV_PAD - CONV_W // 2
        halves = []
        for lanes in (slice(0, D_GROUP // 2), slice(D_GROUP // 2, D_GROUP)):
            acc = jnp.broadcast_to(b_ref[:, lanes], (rt, D_GROUP // 2))
            win = pad_ref[pl.ds(base, rows), lanes]
            for sub in range(SUB):
                shifted = pltpu.roll(win, (rows - sub) % rows, axis=0) if sub else win
                for j in range(CONV_W):
                    off = first + j
                    if off % SUB == sub:
                        acc = acc + dw_ref[j:j + 1, lanes] * shifted[off - sub:off - sub + rt, :]
            halves.append(acc)
        o_ref[pl.ds(base, rt), :] = jnp.concatenate(halves, axis=1)
        return 0

    lax.fori_loop(0, t // rt, tile, 0)


def conv_mix(z, p, bsz, t):
    n = bsz * t
    col = (ZC_OFF - ZREST_OFF) // ZC_W
    vec = _const((1, D_GROUP))
    return pl.pallas_call(
        _conv_kernel,
        grid=(bsz,),
        in_specs=[pl.BlockSpec((t, ZC_W), lambda i: (i, col)), _const((CONV_W, D_GROUP)), vec],
        out_specs=pl.BlockSpec((t, D_GROUP), lambda i: (i, 0)),
        out_shape=jax.ShapeDtypeStruct((n, D_GROUP), F32),
        scratch_shapes=[pltpu.VMEM((t + 2 * CONV_PAD, D_GROUP), F32)],
        compiler_params=_params("parallel"),
        name="conv_mix",
    )(z, p['dw'], p['b'])


def _lane_group(n_lanes):
    return lax.broadcasted_iota(jnp.int32, (1, n_lanes), 1) // HEAD_DIM


def _mlstm_scan_kernel(has_init, cb, *refs):
    if has_init:
        c0_ref, n0_ref, m0_ref = refs[:3]
        refs = refs[3:]
    ins, (h0_ref, h1_ref, cfin_ref, nfin_ref, mfin_ref, c_ref, n_ref, m_ref) = refs[:8], refs[8:]
    L = CHUNK
    nh = N_DIR * N_HEADS
    W = D_GROUP
    i = pl.program_id(1)

    @pl.when(i == 0)
    def _():
        c_ref[...] = jnp.zeros_like(c_ref)
        if has_init:
            for d in range(N_DIR):
                for h in range(N_HEADS):
                    hs = slice(h * HEAD_DIM, (h + 1) * HEAD_DIM)
                    c_ref[d, hs, hs] = c0_ref[0, d, h]
            n_ref[...] = n0_ref[0]
            m_ref[...] = m0_ref[0]
        else:
            n_ref[...] = jnp.zeros_like(n_ref)
            m_ref[...] = jnp.zeros_like(m_ref)

    ones = _head_ones()
    row_head = lax.broadcasted_iota(jnp.int32, (W, 1), 0) // HEAD_DIM
    same_head = row_head == _lane_group(W)
    gate_lane = lax.broadcasted_iota(jnp.int32, (ZG_W, 1), 0)
    pick = (lax.broadcasted_iota(jnp.int32, (nh, ZG_W), 0) == lax.broadcasted_iota(jnp.int32, (nh, ZG_W), 1)).astype(BF16)
    s_pos = lax.broadcasted_iota(jnp.int32, (L, W), 1) % L
    t_pos = lax.broadcasted_iota(jnp.int32, (L, W), 0)
    lane128 = lax.broadcasted_iota(jnp.int32, (L, 128), 1)

    blocks = []
    for d in range(N_DIR):
        reverse = d == 1
        q_ref, k_ref, v_ref, g_ref = ins[4 * d:4 * d + 4]
        spread_i = (gate_lane == d * N_HEADS + _lane_group(W)).astype(BF16)
        spread_b = (gate_lane == nh + d * N_HEADS + _lane_group(W)).astype(BF16)
        for c in range(cb):
            rows = _chunk_rows(c, cb, reverse)
            blocks.append(dict(d=d, c=c, reverse=reverse, last=0 if reverse else L - 1,
                               causal=(s_pos >= t_pos) if reverse else (s_pos <= t_pos),
                               tri=_tri(L, reverse, False).astype(BF16), spread_i=spread_i, spread_b=spread_b,
                               q=_bf(q_ref[rows, :]), k=k_ref[rows, :], v=_bf(v_ref[rows, :]), g=g_ref[rows, :]))
    for u in blocks:
        u['bc_all'] = _cumsum_mm(u['tri'], u['g'])
    for u in blocks:
        u['src'] = u['g'] - pltpu.roll(u['bc_all'], ZG_W - nh, axis=1)
        u['bb'] = sum(_dot(p, u['spread_b']) for p in _split(u['bc_all'], 3))
        u['li'] = sum(_dot(p, u['spread_i']) for p in _split(u['g'], 3))
    for u in blocks:
        rows8 = sum(_dot_nt(pick, p) for p in _split(u['src'], 3))
        j0 = u['d'] * N_HEADS
        u['src_row'] = jnp.concatenate([rows8[j0 + h:j0 + h + 1, :] for h in range(N_HEADS)], axis=1)
        last = u['last']
        u['b_last'] = u['bb'][last:last + 1, :]
        k4 = jnp.concatenate([_bf(u['k'])] * N_HEADS, axis=0)
        v4 = jnp.concatenate([u['v']] * N_HEADS, axis=0)
        u['k_exp'] = jnp.where(same_head, k4, jnp.zeros_like(k4))
        u['v_exp'] = jnp.where(same_head, v4, jnp.zeros_like(v4))
    for u in blocks:
        u['log_w'] = jnp.where(u['causal'], u['bb'] + u['src_row'], -jnp.inf)
        u['qk'] = _dot_nt(u['q'], u['k_exp'])
        u['gl'] = u['b_last'] - u['bb'] + u['li']
    for u in blocks:
        cols = []
        for half in range(2):
            x = u['log_w'][:, half * 128:(half + 1) * 128]
            lo = jnp.max(jnp.where(lane128 < HEAD_DIM, x, -jnp.inf), axis=1, keepdims=True)
            hi = jnp.max(jnp.where(lane128 >= HEAD_DIM, x, -jnp.inf), axis=1, keepdims=True)
            cols.append(jnp.where(lane128 < HEAD_DIM, lo, hi))
        u['a'] = jnp.concatenate(cols, axis=1)
    for u in blocks:
        last = u['last']
        u['a_last'] = u['a'][last:last + 1, :]
        u['s_loc'] = u['qk'] * jnp.exp(u['log_w'] - u['a'])
    for u in blocks:
        u['wk'] = jnp.exp(u['gl'] - u['a_last']) * u['k']
        u['den_loc'] = _head_sum(u['s_loc'], ones)
        u['num_loc'] = _dot(_bf(u['s_loc']), u['v_exp'])
    for u in blocks:
        u['c_loc'] = jnp.where(same_head, _dot_tn(u['v'], _bf(u['wk'])), 0.0)
        u['n_loc'] = jnp.sum(u['wk'], axis=0, keepdims=True)

    cs = [c_ref[d] for d in range(N_DIR)]
    ns = [n_ref[d:d + 1, :] for d in range(N_DIR)]
    ms = [m_ref[d:d + 1, :] for d in range(N_DIR)]
    for c in range(cb):
        now = [u for u in blocks if u['c'] == c]
        for u in now:
            d = u['d']
            u['c_in'], u['n_in'], u['m_in'] = cs[d], ns[d], ms[d]
            u['m_new'] = jnp.maximum(u['b_last'] + ms[d], u['a_last'])
        for u in now:
            u['carry'] = jnp.exp(u['b_last'] + u['m_in'] - u['m_new'])
            u['fresh'] = jnp.exp(u['a_last'] - u['m_new'])
        for u in now:
            d = u['d']
            cs[d] = u['carry'] * cs[d] + u['fresh'] * u['c_loc']
            ns[d] = u['carry'] * ns[d] + u['fresh'] * u['n_loc']
            ms[d] = u['m_new']

    for u in blocks:
        u['qc'] = _dot_nt(u['q'], _bf(u['c_in']))
        u['qn'] = _head_sum(u['q'].astype(F32) * u['n_in'], ones)
        u['m_t'] = jnp.maximum(u['bb'] + u['m_in'], u['a'])
    for u in blocks:
        u['scale'] = jnp.exp(u['a'] - u['m_t'])
        u['inter'] = jnp.exp(u['bb'] + u['m_in'] - u['m_t'])
        u['floor'] = jnp.exp(-u['m_t'])
    for u in blocks:
        u['num'] = u['scale'] * u['num_loc'] + u['inter'] * u['qc']
        u['den'] = jnp.maximum(jnp.abs(u['scale'] * u['den_loc'] + u['inter'] * u['qn']), u['floor'])
    for u in blocks:
        h_ref = h1_ref if u['reverse'] else h0_ref
        h_ref[_chunk_rows(u['c'], cb, u['reverse']), :] = u['num'] / u['den']
    for d in range(N_DIR):
        c_ref[d] = cs[d]
        n_ref[d:d + 1, :] = ns[d]
        m_ref[d:d + 1, :] = ms[d]

    @pl.when(i == pl.num_programs(1) - 1)
    def _():
        for d in range(N_DIR):
            for h in range(N_HEADS):
                hs = slice(h * HEAD_DIM, (h + 1) * HEAD_DIM)
                cfin_ref[0, d, h] = c_ref[d, hs, hs]
        nfin_ref[0] = n_ref[...]
        mfin_ref[0] = m_ref[...]


def mlstm_scan(q, k, z, g, init, bsz, t):
    cb = SCAN_CB
    blk = cb * CHUNK
    assert t % blk == 0
    nc = t // blk
    vcol = (ZD_OFF + 2 * D_GROUP - ZREST_OFF) // D_GROUP

    def specs(row):
        return [pl.BlockSpec((blk, D_GROUP), lambda bi, i: (row(bi, i), 0)),
                pl.BlockSpec((blk, D_GROUP), lambda bi, i: (row(bi, i), 0)),
                pl.BlockSpec((blk, D_GROUP), lambda bi, i: (row(bi, i), vcol)),
                pl.BlockSpec((blk, ZG_W), lambda bi, i: (row(bi, i), 0))]

    fwd = specs(lambda bi, i: bi * nc + i)
    bwd = specs(lambda bi, i: bi * nc + nc - 1 - i)
    cst = pl.BlockSpec((1, N_DIR, N_HEADS, HEAD_DIM, HEAD_DIM), lambda bi, i: (bi, 0, 0, 0, 0))
    vst = pl.BlockSpec((1, N_DIR, D_GROUP), lambda bi, i: (bi, 0, 0))
    has_init = init is not None
    args = (list(init) if has_init else []) + [q, k, z, g] * 2
    tok = jax.ShapeDtypeStruct((bsz * t, D_GROUP), F32)
    vec = jax.ShapeDtypeStruct((bsz, N_DIR, D_GROUP), F32)
    return pl.pallas_call(
        functools.partial(_mlstm_scan_kernel, has_init, cb),
        grid=(bsz, nc),
        in_specs=([cst, vst, vst] if has_init else []) + fwd + bwd,
        out_specs=[fwd[0], bwd[0], cst, vst, vst],
        out_shape=[tok, tok, jax.ShapeDtypeStruct((bsz, N_DIR, N_HEADS, HEAD_DIM, HEAD_DIM), F32), vec, vec],
        scratch_shapes=[pltpu.VMEM((N_DIR, D_GROUP, D_GROUP), F32),
                        pltpu.VMEM((N_DIR, D_GROUP), F32),
                        pltpu.VMEM((N_DIR, D_GROUP), F32)],
        compiler_params=_params("parallel", "arbitrary"),
        name="mlstm_scan",
    )(*args)


def _mix_out(x_ref, mod_ref, y0_ref, y1_ref, bon_ref, g_ref, lng_ref, lnb_ref, pd_ref, pw_ref, ps_ref,
             cv_ref, clg_ref, clb_ref, cpw_ref, h0_ref, h1_ref, o_ref, hng_ref, w_ref):
    ones = _head_ones()
    inv = 1.0 / HEAD_DIM
    y = y0_ref[...] + y1_ref[...] + bon_ref[...]
    mu = _head_sum(y, ones) * inv
    yc = y - mu
    var = _head_sum(yc * yc, ones) * inv
    ya = (yc * lax.rsqrt(var + RWKV_LN_EPS) * lng_ref[...] + lnb_ref[...]) * g_ref[...]
    y_pool = _mm(pd_ref[...], pw_ref[...]) * ps_ref[...]
    cv = cv_ref[...]
    cmu = jnp.mean(cv, axis=-1, keepdims=True)
    cvar = jnp.mean(jnp.square(cv - cmu), axis=-1, keepdims=True)
    cn = (cv - cmu) * lax.rsqrt(cvar + CONV_LN_EPS) * clg_ref[...] + clb_ref[...]
    y_conv = _mm(cn * _sigmoid(cn), cpw_ref[...])
    hs = (h0_ref[...] + h1_ref[...]) * _sigmoid(o_ref[...])
    yd = hs * lax.rsqrt(_head_sum(hs * hs, ones) * inv + EPS) * hng_ref[...]
    g = D_GROUP
    mixed = (_mm(ya, w_ref[0:g, :]) + _mm(y_pool, w_ref[g:2 * g, :])
             + _mm(y_conv, w_ref[2 * g:3 * g, :]) + _mm(yd, w_ref[3 * g:4 * g, :]))
    return x_ref[...] + mod_ref[0, 2:3, :] * mixed


def _mlp(final, x, mod_ref, g_ref, w1_ref, b1_ref, w2_ref, b2_ref, fg_ref):
    h = _rms(x, g_ref[...]) * (1.0 + mod_ref[0, 4:5, :]) + mod_ref[0, 3:4, :]
    a = jnp.maximum(jnp.dot(h.astype(BF16), w1_ref[...], preferred_element_type=F32) + b1_ref[...], 0.0)
    f = jnp.dot((a * a).astype(BF16), w2_ref[...], preferred_element_type=F32) + b2_ref[...]
    x = x + mod_ref[0, 5:6, :] * f
    return _rms(x, fg_ref[...]) if final else x


def _mix_mlp_kernel(final, *refs):
    mix_refs, mlp_refs, out_ref = refs[:20], refs[20:26], refs[26]
    x = _mix_out(*mix_refs)
    out_ref[...] = _mlp(final, x, mix_refs[1], *mlp_refs)


def mix_mlp(x, mod, ra, pooled, conv, md, z, p, final_g, t, final):
    n = x.shape[0]
    tok = pl.BlockSpec((ROW_TILE, D_GROUP), lambda i: (i, 0))
    ocol = (ZD_OFF + 3 * D_GROUP - ZREST_OFF) // D_GROUP
    vec = _const((1, D_GROUP))
    wide = _const((1, D_MODEL))
    sq = _const((D_GROUP, D_GROUP))
    row = pl.BlockSpec((ROW_TILE, D_MODEL), lambda i: (i, 0))
    return pl.pallas_call(
        functools.partial(_mix_mlp_kernel, final),
        grid=(n // ROW_TILE,),
        in_specs=[row, _mod_spec(mod.shape[0] > 1, t), tok, tok, tok, tok, vec, vec, tok, sq, vec,
                  tok, vec, vec, sq, tok, tok,
                  pl.BlockSpec((ROW_TILE, D_GROUP), lambda i: (i, ocol)), vec, _const((D_MODEL, D_MODEL)),
                  wide, _const((D_MODEL, D_FF)), _const((1, D_FF)), _const((D_FF, D_MODEL)), wide, wide],
        out_specs=row,
        out_shape=jax.ShapeDtypeStruct((n, D_MODEL), F32),
        compiler_params=_params("parallel"),
        name="mix_mlp",
    )(x, mod, ra['y0'], ra['y1'], ra['bon'], ra['g'], p['rwkv_ln_g'], p['rwkv_ln_b'],
      pooled, p['pool_w'], p['pool_scale'], conv, p['conv']['ln_g'], p['conv']['ln_b'], p['conv']['pw'],
      md['h0'], md['h1'], z, p['hn_g'], p['w_out'],
      p['norm2_g'], p['w1'], p['b1'], p['w2'], p['b2'], final_g)


def _trunk_layer(x, mod, p, states, bsz, t, grid, final_g, final):
    z, r, kk, v, g, bon, lw0, lw1, kd0, kd1, b0, b1, q, k, gates = in_proj(x, mod, p, t)
    y0, y1, s_rwkv = rwkv_scan(r, kk, v, (lw0, lw1), (kd0, kd1), (b0, b1),
                               None if states is None else states[0], bsz, t)
    yb = pool_mix(z, bsz, t, grid)
    yc = conv_mix(z, p['conv'], bsz, t)
    h0, h1, s_c, s_n, s_m = mlstm_scan(q, k, z, gates, None if states is None else states[1:], bsz, t)
    x = mix_mlp(x, mod, dict(y0=y0, y1=y1, bon=bon, g=g), yb, yc, dict(h0=h0, h1=h1), z, p, final_g, t, final)
    return x, (s_rwkv, s_c, s_n, s_m)


def _layer_params(l, norm1_g, norm2_g, w_in, w_out, rwkv_mu, rwkv_w0, rwkv_w_up, rwkv_a0, rwkv_a_up, rwkv_g_up,
                  rwkv_k_k, rwkv_k_a, rwkv_r_k, rwkv_ln_g, rwkv_ln_b, pool_w, pool_scale, conv_dw, conv_b,
                  conv_ln_g, conv_ln_b, conv_pw, mlstm_qk_conv, mlstm_i_bias, mlstm_f_bias, mlstm_hn_g,
                  mlp_w1, mlp_b1, mlp_w2, mlp_b2):
    row = lambda a: a[l].reshape(1, -1)
    w = w_in[l]
    a_w, b_w, c_w = 1024, 256, 512
    wa, wb, wc, wd = w[:, :a_w], w[:, a_w:a_w + b_w], w[:, a_w + b_w:a_w + b_w + c_w], w[:, a_w + b_w + c_w:]
    nh = N_DIR * N_HEADS
    qkv, gates, o = wd[:, :3 * D_GROUP], wd[:, 3 * D_GROUP:3 * D_GROUP + 2 * nh], wd[:, 3 * D_GROUP + 2 * nh:]
    w_cat = jnp.concatenate([wa, qkv, o, wc, wb, gates, jnp.zeros((D_MODEL, ZG_W - 2 * nh), F32)], axis=1)
    rank = rwkv_w_up.shape[2]
    zeros = jnp.zeros((N_DIR, rank, D_GROUP), F32)
    pool_bd = jax.scipy.linalg.block_diag(*[pool_w[l, g] for g in range(len(POOL_HALF))])
    gate_bias = jnp.concatenate([mlstm_i_bias[l], mlstm_f_bias[l], jnp.zeros((ZG_W - 2 * nh,), F32)]).reshape(1, ZG_W)
    return dict(
        norm1_g=row(norm1_g), norm2_g=row(norm2_g), w_in=w_cat.astype(BF16), w_out=w_out[l].astype(BF16),
        rwkv=dict(mu=row(rwkv_mu), k_k=row(rwkv_k_k), k_a=row(rwkv_k_a), r_k=row(rwkv_r_k),
                  w0=rwkv_w0[l].reshape(N_DIR, 1, D_GROUP), a0=rwkv_a0[l].reshape(N_DIR, 1, D_GROUP),
                  w_up=jnp.concatenate([rwkv_w_up[l], zeros], axis=1).astype(BF16),
                  a_up=jnp.concatenate([zeros, rwkv_a_up[l]], axis=1).astype(BF16),
                  g_up=rwkv_g_up[l].astype(BF16)),
        rwkv_ln_g=row(rwkv_ln_g), rwkv_ln_b=row(rwkv_ln_b),
        pool_w=pool_bd.astype(BF16), pool_scale=row(pool_scale),
        conv=dict(dw=conv_dw[l], b=row(conv_b), ln_g=row(conv_ln_g), ln_b=row(conv_ln_b), pw=conv_pw[l].astype(BF16)),
        mlstm=dict(qk_conv=mlstm_qk_conv[l], gate_bias=gate_bias),
        hn_g=row(mlstm_hn_g),
        w1=mlp_w1[l].astype(BF16), b1=row(mlp_b1), w2=mlp_w2[l].astype(BF16), b2=row(mlp_b2),
    )


def kernel(x_prompt, x_sample, c, state_rwkv, state_mlstm_C, state_mlstm_n, state_mlstm_m, c_ctx, w_mod, b_mod, norm1_g, norm2_g, w_in, w_out, rwkv_mu, rwkv_w0, rwkv_w_up, rwkv_a0, rwkv_a_up, rwkv_g_up, rwkv_k_k, rwkv_k_a, rwkv_r_k, rwkv_ln_g, rwkv_ln_b, pool_w, pool_scale, conv_dw, conv_b, conv_ln_g, conv_ln_b, conv_pw, mlstm_qk_conv, mlstm_i_bias, mlstm_f_bias, mlstm_hn_g, mlp_w1, mlp_b1, mlp_w2, mlp_b2, final_g):
    bp, tp, d = x_prompt.shape
    bs, ts, _ = x_sample.shape
    nh = N_DIR * N_HEADS
    cvec = jnp.concatenate([c_ctx[None, :], c, jnp.zeros((SUB - 1 - bs, d), F32)], axis=0)
    mods = modulation(cvec, w_mod, b_mod).reshape(DEPTH, SUB, 6, d)
    xp = x_prompt.reshape(bp * tp, d)
    xs = x_sample.reshape(bs * ts, d)
    fg = final_g.reshape(1, d)
    new = [[], [], [], []]
    for l in range(DEPTH):
        p = _layer_params(l, norm1_g, norm2_g, w_in, w_out, rwkv_mu, rwkv_w0, rwkv_w_up, rwkv_a0, rwkv_a_up,
                          rwkv_g_up, rwkv_k_k, rwkv_k_a, rwkv_r_k, rwkv_ln_g, rwkv_ln_b, pool_w, pool_scale,
                          conv_dw, conv_b, conv_ln_g, conv_ln_b, conv_pw, mlstm_qk_conv, mlstm_i_bias,
                          mlstm_f_bias, mlstm_hn_g, mlp_w1, mlp_b1, mlp_w2, mlp_b2)
        final = l == DEPTH - 1
        xp, st = _trunk_layer(xp, mods[l, 0:1], p, None, bp, tp, False, fg, final)
        for acc, s in zip(new, st):
            acc.append(s)
        cached = (state_rwkv[:, l], state_mlstm_C[:, l],
                  state_mlstm_n[:, l].reshape(bs, N_DIR, D_GROUP),
                  jnp.repeat(state_mlstm_m[:, l], HEAD_DIM, axis=-1))
        xs, _ = _trunk_layer(xs, mods[l, 1:1 + bs], p, cached, bs, ts, True, fg, final)
    return (xp.reshape(bp, tp, d), xs.reshape(bs, ts, d),
            jnp.stack(new[0], axis=1), jnp.stack(new[1], axis=1),
            jnp.stack(new[2], axis=1).reshape(bp, DEPTH, N_DIR, N_HEADS, HEAD_DIM),
            jnp.stack(new[3], axis=1).reshape(bp, DEPTH, N_DIR, N_HEADS, HEAD_DIM)[..., 0])
```

```python
import functools
import math

import jax
import jax.numpy as jnp
from jax import lax
from jax.experimental import pallas as pl
from jax.experimental.pallas import tpu as pltpu

F32 = jnp.float32
BF16 = jnp.bfloat16

D_MODEL = 1024
D_GROUP = 256
HEAD_DIM = 64
N_HEADS = 4
N_DIR = 2
DEPTH = 4
GRID_W = 64
POOL_HALF = (1, 2, 4, 8)
MAX_HALF = POOL_HALF[-1]
CONV_W = 31
CONV_PAD = 16
D_FF = 4 * D_MODEL
EPS = 1e-6
RWKV_LN_EPS = 64e-5
CONV_LN_EPS = 1e-5
CHUNK = 64
ROW_TILE = 512
SUB = 8
VMEM_LIMIT = 56 * 1024 * 1024

Z_W = 2944
ZA_W, ZD_QK_W, ZC_W, ZB_W, ZG_W = 1024, 512, 512, 256, 128
ZD_OFF, ZC_OFF, ZB_OFF, ZG_OFF = 1024, 2048, 2560, 2816
ZREST_OFF, ZREST_W = 1536, 1280


def _mm(a, b):
    return jnp.dot(a.astype(BF16), b.astype(BF16), preferred_element_type=F32)


def _split(x, parts):
    out = []
    for _ in range(parts):
        p = x.astype(BF16)
        out.append(p)
        x = x - p.astype(F32)
    return out


def _head_sum(x, ones):
    return sum(jnp.dot(p, ones, preferred_element_type=F32) for p in _split(x, 2))


def _cumsum_mm(tri, x):
    return sum(jnp.dot(tri, p, preferred_element_type=F32) for p in _split(x, 3))


def _tri(n, reverse, strict):
    t = lax.broadcasted_iota(jnp.int32, (n, n), 0)
    s = lax.broadcasted_iota(jnp.int32, (n, n), 1)
    if reverse:
        return (s > t) if strict else (s >= t)
    return (s < t) if strict else (s <= t)


def _head_ones():
    a = lax.broadcasted_iota(jnp.int32, (D_GROUP, D_GROUP), 0) // HEAD_DIM
    b = lax.broadcasted_iota(jnp.int32, (D_GROUP, D_GROUP), 1) // HEAD_DIM
    return (a == b).astype(BF16)


def _sigmoid(x):
    return jax.nn.sigmoid(x)


def _softplus(x):
    return jnp.maximum(x, 0.0) + jnp.log1p(jnp.exp(-jnp.abs(x)))


def _rms(x, g):
    return x * lax.rsqrt(jnp.mean(x * x, axis=-1, keepdims=True) + EPS) * g


def _params(*sem):
    return pltpu.CompilerParams(dimension_semantics=sem, vmem_limit_bytes=VMEM_LIMIT)


def _const(shape):
    return pl.BlockSpec(shape, lambda *_: (0,) * len(shape), pipeline_mode=pl.Buffered(1))


def _halo_specs(width, col, n_rows, tt):
    per = tt // SUB
    last = n_rows // SUB - 1
    prev = pl.BlockSpec((SUB, width), lambda i: (jnp.maximum(i * per - 1, 0), col))
    nxt = pl.BlockSpec((SUB, width), lambda i: (jnp.minimum((i + 1) * per, last), col))
    return prev, nxt


def _mod_kernel(c_ref, w_ref, b_ref, o_ref):
    c = c_ref[...]
    o_ref[0] = _mm(c * _sigmoid(c), w_ref[0]) + b_ref[0]


def modulation(cvec, w_mod, b_mod):
    tn = 1536
    n = w_mod.shape[-1]
    return pl.pallas_call(
        _mod_kernel,
        grid=(DEPTH, n // tn),
        in_specs=[pl.BlockSpec((SUB, D_MODEL), lambda l, j: (0, 0)),
                  pl.BlockSpec((1, D_MODEL, tn), lambda l, j: (l, 0, j)),
                  pl.BlockSpec((1, 1, tn), lambda l, j: (l, 0, j))],
        out_specs=pl.BlockSpec((1, SUB, tn), lambda l, j: (l, 0, j)),
        out_shape=jax.ShapeDtypeStruct((DEPTH, SUB, n), F32),
        compiler_params=_params("parallel", "parallel"),
        name="modulation",
    )(cvec, w_mod, b_mod.reshape(DEPTH, 1, n))


def _mod_spec(per_batch, rows_per_seq):
    if per_batch:
        return pl.BlockSpec((1, 6, D_MODEL), lambda i: (i * ROW_TILE // rows_per_seq, 0, 0))
    return pl.BlockSpec((1, 6, D_MODEL), lambda i: (0, 0, 0))


def _shifted(ze, i, t):
    prev = ze[SUB - 1:SUB - 1 + ROW_TILE, :]
    nxt = ze[SUB + 1:SUB + 1 + ROW_TILE, :]
    pos = (i * ROW_TILE + lax.broadcasted_iota(jnp.int32, (ROW_TILE, 1), 0)) % t
    return jnp.where(pos == 0, 0.0, prev), jnp.where(pos == t - 1, 0.0, nxt)


def _in_proj_kernel(t, x_ref, xp_ref, xn_ref, mod_ref, g_ref, w_ref,
                    mu_ref, kk_ref, ka_ref, rk_ref, w0_ref, a0_ref, wup_ref, aup_ref, gup_ref, cw_ref, gb_ref,
                    z_o, r_o, kk_o, v_o, g_o, bon_o, lw0_o, lw1_o, kd0_o, kd1_o, b0_o, b1_o, q_o, k_o, gate_o):
    i = pl.program_id(0)
    xe = jnp.concatenate([xp_ref[...], x_ref[...], xn_ref[...]], axis=0)
    h = _rms(xe, g_ref[...]) * (1.0 + mod_ref[0, 1:2, :]) + mod_ref[0, 0:1, :]
    ze = jnp.dot(h.astype(BF16), w_ref[...], preferred_element_type=F32)
    mid = slice(SUB, SUB + ROW_TILE)
    z_o[...] = ze[mid, ZREST_OFF:ZREST_OFF + ZREST_W]

    za = ze[:, 0:ZA_W]
    z = za[mid, :]
    prev, nxt = _shifted(za, i, t)
    zm = z + (0.5 * (prev + nxt) - z) * mu_ref[...]
    r, k, v = zm[:, 0:256], zm[:, 256:512], zm[:, 512:768]
    lora = zm[:, 768:896]
    gd = zm[:, 896:1024]
    ones = _head_ones()
    kk = k * kk_ref[...]
    kk = kk * lax.rsqrt(_head_sum(kk * kk, ones) + 1e-12)
    wt = jnp.tanh(lora)
    r_o[...] = r
    kk_o[...] = kk
    v_o[...] = v
    g_o[...] = _mm(_sigmoid(gd), gup_ref[...])
    bon = jnp.zeros_like(r)
    for d, (lw_o, kd_o, b_o) in enumerate(((lw0_o, kd0_o, b0_o), (lw1_o, kd1_o, b1_o))):
        lw_o[...] = -math.exp(-0.5) * _sigmoid(w0_ref[d] + _mm(wt, wup_ref[d]))
        a = _sigmoid(a0_ref[d] + _mm(lora, aup_ref[d]))
        kd = k * (1.0 + (a - 1.0) * ka_ref[...])
        kd_o[...] = kd
        b_o[...] = -a * kk
        bon = bon + _head_sum(r * kd * rk_ref[...], ones) * v
    bon_o[...] = bon

    zq = ze[:, ZD_OFF:ZD_OFF + ZD_QK_W]
    prev, nxt = _shifted(zq, i, t)
    qk = cw_ref[0:1, :] * prev + cw_ref[1:2, :] * zq[mid, :] + cw_ref[2:3, :] * nxt
    qk = qk * _sigmoid(qk)
    q_o[...] = qk[:, :D_GROUP]
    k_o[...] = qk[:, D_GROUP:] * (1.0 / math.sqrt(HEAD_DIM))
    gb = ze[mid, ZG_OFF:ZG_OFF + ZG_W] + gb_ref[...]
    lane = lax.broadcasted_iota(jnp.int32, gb.shape, 1)
    gate_o[...] = jnp.where(lane < N_DIR * N_HEADS, gb, -_softplus(-gb))


def in_proj(x, mod, p, t):
    n = x.shape[0]
    tile = pl.BlockSpec((ROW_TILE, D_MODEL), lambda i: (i, 0))
    prev, nxt = _halo_specs(D_MODEL, 0, n, ROW_TILE)
    out = pl.BlockSpec((ROW_TILE, D_GROUP), lambda i: (i, 0))
    vec = _const((1, D_GROUP))
    rw, ml = p['rwkv'], p['mlstm']
    tok = jax.ShapeDtypeStruct((n, D_GROUP), F32)
    return pl.pallas_call(
        functools.partial(_in_proj_kernel, t),
        grid=(n // ROW_TILE,),
        in_specs=[tile, prev, nxt, _mod_spec(mod.shape[0] > 1, t), _const((1, D_MODEL)), _const((D_MODEL, Z_W)),
                  _const((1, ZA_W)), vec, vec, vec,
                  _const((N_DIR, 1, D_GROUP)), _const((N_DIR, 1, D_GROUP)),
                  _const((N_DIR, 128, D_GROUP)), _const((N_DIR, 128, D_GROUP)), _const((128, D_GROUP)),
                  _const((3, ZD_QK_W)), _const((1, ZG_W))],
        out_specs=[pl.BlockSpec((ROW_TILE, ZREST_W), lambda i: (i, 0))] + [out] * 13
                  + [pl.BlockSpec((ROW_TILE, ZG_W), lambda i: (i, 0))],
        out_shape=[jax.ShapeDtypeStruct((n, ZREST_W), F32)] + [tok] * 13 + [jax.ShapeDtypeStruct((n, ZG_W), F32)],
        compiler_params=_params("parallel"),
        name="in_proj",
    )(x, x, x, mod, p['norm1_g'], p['w_in'], rw['mu'], rw['k_k'], rw['k_a'], rw['r_k'], rw['w0'], rw['a0'],
      rw['w_up'], rw['a_up'], rw['g_up'], ml['qk_conv'], ml['gate_bias'])


SCAN_CB = 4


def _bf(x):
    return x.astype(BF16)


def _dot(a, b):
    return jnp.dot(a, b, preferred_element_type=F32)


def _dot_nt(a, b):
    return lax.dot_general(a, b, (((1,), (1,)), ((), ())), preferred_element_type=F32)


def _dot_tn(a, b):
    return lax.dot_general(a, b, (((0,), (0,)), ((), ())), preferred_element_type=F32)


def _tri2(reverse, strict):
    t = lax.broadcasted_iota(jnp.int32, (CHUNK, 2 * CHUNK), 0)
    s = lax.broadcasted_iota(jnp.int32, (CHUNK, 2 * CHUNK), 1) % CHUNK
    if reverse:
        return (s > t) if strict else (s >= t)
    return (s < t) if strict else (s <= t)


def _chunk_rows(c, cb, reverse):
    return slice((cb - 1 - c) * CHUNK, (cb - c) * CHUNK) if reverse else slice(c * CHUNK, (c + 1) * CHUNK)


def _rwkv_scan_kernel(has_init, cb, *refs):
    if has_init:
        s0_ref, refs = refs[0], refs[1:]
    ins, (y0_ref, y1_ref, sfin_ref, s_ref) = refs[:12], refs[12:]
    L = CHUNK
    i = pl.program_id(1)

    @pl.when(i == 0)
    def _():
        if has_init:
            s_ref[...] = s0_ref[0]
        else:
            s_ref[...] = jnp.zeros_like(s_ref)

    eye = (lax.broadcasted_iota(jnp.int32, (L, L), 0) == lax.broadcasted_iota(jnp.int32, (L, L), 1)).astype(F32)
    heads = [slice(h * HEAD_DIM, (h + 1) * HEAD_DIM) for h in range(N_HEADS)]

    blocks = []
    for d in range(N_DIR):
        reverse = d == 1
        incl = _tri(L, reverse, False)
        for c in range(cb):
            rows = _chunk_rows(c, cb, reverse)
            blocks.append(dict(d=d, c=c, rows=rows, reverse=reverse, incl=incl, lw=ins[6 * d + 3][rows, :]))
    for u in blocks:
        u['cum'] = _cumsum_mm(u['incl'].astype(BF16), u['lw'])
    inst = []
    for u in blocks:
        d, rows, cum, lw = u['d'], u['rows'], u['cum'], u['lw']
        r_ref, kk_ref, v_ref, _, kd_ref, b_ref = ins[6 * d:6 * d + 6]
        r, kk, v, kd, b = (ref[rows, :] for ref in (r_ref, kk_ref, v_ref, kd_ref, b_ref))
        last = cum[0:1, :] if u['reverse'] else cum[L - 1:L, :]
        e_neg = jnp.exp(-cum)
        e_end = jnp.exp(last - cum)
        e_all = jnp.exp(last)
        aq = _bf(kk * jnp.exp(cum - lw))
        rq = r * jnp.exp(cum)
        bk = _bf(b * e_neg)
        kq = _bf(kd * e_neg)
        k_end = _bf(kd * e_end)
        b_end = _bf(b * e_end)
        rq_b, v_b = _bf(rq), _bf(v)
        for h, hs in enumerate(heads):
            inst.append(dict(d=d, c=u['c'], h=h, reverse=u['reverse'], aq=aq[:, hs],
                             rq=rq[:, hs], rq_b=rq_b[:, hs], bk=bk[:, hs], kq=kq[:, hs], k_end=k_end[:, hs],
                             b_end=b_end[:, hs], v=v_b[:, hs], e_all=e_all[:, hs]))

    zeros_h = jnp.zeros((L, HEAD_DIM), BF16)
    for t in inst:
        res = _dot_nt(jnp.concatenate([t['aq'], t['rq_b']], axis=0), jnp.concatenate([t['bk'], t['kq']], axis=0))
        t['mn'] = jnp.where(_tri2(t['reverse'], True), res[:L], 0.0)
        t['pp'] = _bf(jnp.where(_tri2(t['reverse'], False), res[L:], 0.0))
        t['m'] = t['mn'][:, :L]
    row = lax.broadcasted_iota(jnp.int32, (L, L), 0)
    col = lax.broadcasted_iota(jnp.int32, (L, L), 1)
    sizes = [2 ** e for e in range(int(math.log2(L)))]
    corner = {s: jnp.logical_and(row // (2 * s) == col // (2 * s), row // s != col // s) for s in sizes}
    for t in inst:
        t['y'] = eye + jnp.where(corner[1], t['m'], 0.0)
    for s in sizes[1:]:
        for t in inst:
            t['my'] = _bf(_dot(_bf(jnp.where(corner[s], t['m'], 0.0)), _bf(t['y'])))
        for t in inst:
            t['y'] = t['y'] + _dot(_bf(t['y']), t['my'])
    for t in inst:
        t['y'] = _bf(t['y'])
        t['nv'] = _bf(_dot(_bf(t['mn']), jnp.concatenate([zeros_h, t['v']], axis=0)))
    for t in inst:
        t['wq'] = _bf(_dot(t['y'], jnp.concatenate([t['nv'], t['aq']], axis=1)))
    for t in inst:
        x = _dot(t['pp'], jnp.concatenate([t['wq'], jnp.concatenate([t['v'], zeros_h], axis=1)], axis=0))
        t['y_loc'] = x[:, :L]
        t['r2'] = _bf(t['rq'] + x[:, L:])
        wb = _dot_tn(t['wq'], t['b_end'])
        t['tq'] = _bf(wb[L:, :])
        t['g'] = _dot_tn(t['v'], t['k_end']) + wb[:L, :]

    state = [[s_ref[d, h] for h in range(N_HEADS)] for d in range(N_DIR)]
    outs = {}
    for c in range(cb):
        now = [t for t in inst if t['c'] == c]
        for t in now:
            s0 = state[t['d']][t['h']]
            s0_b = _bf(s0)
            outs[(t['d'], c, t['h'])] = t['y_loc'] + _dot_nt(t['r2'], s0_b)
            t['s1'] = s0 * t['e_all'] + _dot(s0_b, t['tq']) + t['g']
        for t in now:
            state[t['d']][t['h']] = t['s1']
    for d, y_ref in enumerate((y0_ref, y1_ref)):
        for c in range(cb):
            y_ref[_chunk_rows(c, cb, d == 1), :] = jnp.concatenate([outs[(d, c, h)] for h in range(N_HEADS)], axis=1)
    for d in range(N_DIR):
        for h in range(N_HEADS):
            s_ref[d, h] = state[d][h]

    @pl.when(i == pl.num_programs(1) - 1)
    def _():
        sfin_ref[0] = s_ref[...]


def rwkv_scan(r, kk, v, lw, kd, b, s0, bsz, t):
    cb = SCAN_CB
    blk = cb * CHUNK
    assert t % blk == 0
    nc = t // blk
    fwd = pl.BlockSpec((blk, D_GROUP), lambda bi, i: (bi * nc + i, 0))
    bwd = pl.BlockSpec((blk, D_GROUP), lambda bi, i: (bi * nc + nc - 1 - i, 0))
    st = pl.BlockSpec((1, N_DIR, N_HEADS, HEAD_DIM, HEAD_DIM), lambda bi, i: (bi, 0, 0, 0, 0))
    has_init = s0 is not None
    args = ([s0] if has_init else []) + [r, kk, v, lw[0], kd[0], b[0], r, kk, v, lw[1], kd[1], b[1]]
    in_specs = ([st] if has_init else []) + [fwd] * 6 + [bwd] * 6
    tok = jax.ShapeDtypeStruct((bsz * t, D_GROUP), F32)
    return pl.pallas_call(
        functools.partial(_rwkv_scan_kernel, has_init, cb),
        grid=(bsz, nc),
        in_specs=in_specs,
        out_specs=[fwd, bwd, st],
        out_shape=[tok, tok, jax.ShapeDtypeStruct((bsz, N_DIR, N_HEADS, HEAD_DIM, HEAD_DIM), F32)],
        scratch_shapes=[pltpu.VMEM((N_DIR, N_HEADS, HEAD_DIM, HEAD_DIM), F32)],
        compiler_params=_params("parallel", "arbitrary"),
        name="rwkv_scan",
    )(*args)


def _shift_rows(x, k, pos, n):
    rows = x.shape[0]
    rolled = pltpu.roll(x, k % rows, axis=0)
    ok = (pos >= k) if k > 0 else (pos < n + k)
    return jnp.where(ok, rolled, 0.0)


def _window_sums(x, pos, n):
    fw = x
    bw = _shift_rows(x, 1, pos, n)
    out = [fw + bw]
    for h in POOL_HALF[:-1]:
        fw = fw + _shift_rows(fw, -h, pos, n)
        bw = bw + _shift_rows(bw, h, pos, n)
        out.append(fw + bw)
    return out


def _by_group(vals, lane):
    out = vals[-1]
    for g in range(len(vals) - 2, -1, -1):
        out = jnp.where(lane < (g + 1) * HEAD_DIM, vals[g], out)
    return out


def _clipped_count(pos, h, n):
    return jnp.minimum(pos + h, n) - jnp.maximum(pos - h, 0)


def _pool_seq_kernel(x_ref, o_ref):
    x = x_ref[...]
    t = x.shape[0]
    pos = lax.broadcasted_iota(jnp.int32, x.shape, 0)
    lane = lax.broadcasted_iota(jnp.int32, x.shape, 1)
    sums = _by_group(_window_sums(x, pos, t), lane)
    cnt = _by_group([_clipped_count(pos, h, t) for h in POOL_HALF], lane).astype(F32)
    o_ref[...] = sums / cnt - x


def _pool_grid_kernel(x_ref, o_ref, cs_ref):
    gw = GRID_W
    rows = x_ref.shape[0] // gw
    shape = (gw, D_GROUP)
    col = lax.broadcasted_iota(jnp.int32, shape, 0)
    lane = lax.broadcasted_iota(jnp.int32, shape, 1)
    half = _by_group([jnp.full(shape, h, jnp.int32) for h in POOL_HALF], lane)
    cnt_c = _by_group([_clipped_count(col, h, gw) for h in POOL_HALF], lane)

    def col_stage(r, _):
        at = pl.ds(pl.multiple_of(r * gw, gw), gw)
        cs_ref[at, :] = _by_group(_window_sums(x_ref[at, :], col, gw), lane)
        return 0

    lax.fori_loop(0, rows, col_stage, 0)

    def row_stage(r, _):
        at = pl.ds(pl.multiple_of(r * gw, gw), gw)
        acc = jnp.zeros(shape, F32)
        for o in range(-MAX_HALF, MAX_HALF):
            rr = r + o
            src = pl.ds(pl.multiple_of(jnp.clip(rr, 0, rows - 1) * gw, gw), gw)
            in_window = (half > o) if o >= 0 else (half >= -o)
            ok = jnp.logical_and(in_window, jnp.logical_and(rr >= 0, rr < rows))
            acc = acc + jnp.where(ok, cs_ref[src, :], 0.0)
        cnt = (_clipped_count(r, half, rows) * cnt_c).astype(F32)
        o_ref[at, :] = acc / cnt - x_ref[at, :]
        return 0

    lax.fori_loop(0, rows, row_stage, 0)


def pool_mix(z, bsz, t, grid):
    n = bsz * t
    col = (ZB_OFF - ZREST_OFF) // ZB_W
    scratch = [pltpu.VMEM((t, D_GROUP), F32)] if grid else []
    return pl.pallas_call(
        _pool_grid_kernel if grid else _pool_seq_kernel,
        grid=(bsz,),
        in_specs=[pl.BlockSpec((t, ZB_W), lambda i: (i, col))],
        out_specs=pl.BlockSpec((t, D_GROUP), lambda i: (i, 0)),
        out_shape=jax.ShapeDtypeStruct((n, D_GROUP), F32),
        scratch_shapes=scratch,
        compiler_params=_params("parallel"),
        name="pool_grid" if grid else "pool_seq",
    )(z)


CONV_TILE = 64


def _conv_kernel(z_ref, dw_ref, b_ref, o_ref, pad_ref):
    t = z_ref.shape[0]
    rt = CONV_TILE
    edge = jnp.zeros((CONV_PAD, D_GROUP), F32)
    pad_ref[0:CONV_PAD, :] = edge
    pad_ref[t + CONV_PAD:t + 2 * CONV_PAD, :] = edge

    def glu(i, _):
        base = pl.multiple_of(i * rt, rt)
        zz = z_ref[pl.ds(base, rt), :]
        pad_ref[pl.ds(base + CONV_PAD, rt), :] = zz[:, :D_GROUP] * _sigmoid(zz[:, D_GROUP:])
        return 0

    lax.fori_loop(0, t // rt, glu, 0)

    def tile(i, _):
        base = pl.multiple_of(i * rt, rt)
        rows = rt + 2 * CONV_PAD
        first = CONV_PAD - CONV_W // 2
        halves = []
        for lanes in (slice(0, D_GROUP // 2), slice(D_GROUP // 2, D_GROUP)):
            acc = jnp.broadcast_to(b_ref[:, lanes], (rt, D_GROUP // 2))
            win = pad_ref[pl.ds(base, rows), lanes]
            for sub in range(SUB):
                shifted = pltpu.roll(win, (rows - sub) % rows, axis=0) if sub else win
                for j in range(CONV_W):
                    off = first + j
                    if off % SUB == sub:
                        acc = acc + dw_ref[j:j + 1, lanes] * shifted[off - sub:off - sub + rt, :]
            halves.append(acc)
        o_ref[pl.ds(base, rt), :] = jnp.concatenate(halves, axis=1)
        return 0

    lax.fori_loop(0, t // rt, tile, 0)


def conv_mix(z, p, bsz, t):
    n = bsz * t
    col = (ZC_OFF - ZREST_OFF) // ZC_W
    vec = _const((1, D_GROUP))
    return pl.pallas_call(
        _conv_kernel,
        grid=(bsz,),
        in_specs=[pl.BlockSpec((t, ZC_W), lambda i: (i, col)), _const((CONV_W, D_GROUP)), vec],
        out_specs=pl.BlockSpec((t, D_GROUP), lambda i: (i, 0)),
        out_shape=jax.ShapeDtypeStruct((n, D_GROUP), F32),
        scratch_shapes=[pltpu.VMEM((t + 2 * CONV_PAD, D_GROUP), F32)],
        compiler_params=_params("parallel"),
        name="conv_mix",
    )(z, p['dw'], p['b'])


def _lane_group(n_lanes):
    return lax.broadcasted_iota(jnp.int32, (1, n_lanes), 1) // HEAD_DIM


def _mlstm_scan_kernel(has_init, cb, *refs):
    if has_init:
        c0_ref, n0_ref, m0_ref = refs[:3]
        refs = refs[3:]
    ins, (h0_ref, h1_ref, cfin_ref, nfin_ref, mfin_ref, c_ref, n_ref, m_ref) = refs[:8], refs[8:]
    L = CHUNK
    nh = N_DIR * N_HEADS
    W = D_GROUP
    i = pl.program_id(1)

    @pl.when(i == 0)
    def _():
        c_ref[...] = jnp.zeros_like(c_ref)
        if has_init:
            for d in range(N_DIR):
                for h in range(N_HEADS):
                    hs = slice(h * HEAD_DIM, (h + 1) * HEAD_DIM)
                    c_ref[d, hs, hs] = c0_ref[0, d, h]
            n_ref[...] = n0_ref[0]
            m_ref[...] = m0_ref[0]
        else:
            n_ref[...] = jnp.zeros_like(n_ref)
            m_ref[...] = jnp.zeros_like(m_ref)

    ones = _head_ones()
    row_head = lax.broadcasted_iota(jnp.int32, (W, 1), 0) // HEAD_DIM
    same_head = row_head == _lane_group(W)
    gate_lane = lax.broadcasted_iota(jnp.int32, (ZG_W, 1), 0)
    pick = (lax.broadcasted_iota(jnp.int32, (nh, ZG_W), 0) == lax.broadcasted_iota(jnp.int32, (nh, ZG_W), 1)).astype(BF16)
    s_pos = lax.broadcasted_iota(jnp.int32, (L, W), 1) % L
    t_pos = lax.broadcasted_iota(jnp.int32, (L, W), 0)
    lane128 = lax.broadcasted_iota(jnp.int32, (L, 128), 1)

    blocks = []
    for d in range(N_DIR):
        reverse = d == 1
        q_ref, k_ref, v_ref, g_ref = ins[4 * d:4 * d + 4]
        spread_i = (gate_lane == d * N_HEADS + _lane_group(W)).astype(BF16)
        spread_b = (gate_lane == nh + d * N_HEADS + _lane_group(W)).astype(BF16)
        for c in range(cb):
            rows = _chunk_rows(c, cb, reverse)
            blocks.append(dict(d=d, c=c, reverse=reverse, last=0 if reverse else L - 1,
                               causal=(s_pos >= t_pos) if reverse else (s_pos <= t_pos),
                               tri=_tri(L, reverse, False).astype(BF16), spread_i=spread_i, spread_b=spread_b,
                               q=_bf(q_ref[rows, :]), k=k_ref[rows, :], v=_bf(v_ref[rows, :]), g=g_ref[rows, :]))
    for u in blocks:
        u['bc_all'] = _cumsum_mm(u['tri'], u['g'])
    for u in blocks:
        u['src'] = u['g'] - pltpu.roll(u['bc_all'], ZG_W - nh, axis=1)
        u['bb'] = sum(_dot(p, u['spread_b']) for p in _split(u['bc_all'], 3))
        u['li'] = sum(_dot(p, u['spread_i']) for p in _split(u['g'], 3))
    for u in blocks:
        rows8 = sum(_dot_nt(pick, p) for p in _split(u['src'], 3))
        j0 = u['d'] * N_HEADS
        u['src_row'] = jnp.concatenate([rows8[j0 + h:j0 + h + 1, :] for h in range(N_HEADS)], axis=1)
        last = u['last']
        u['b_last'] = u['bb'][last:last + 1, :]
        k4 = jnp.concatenate([_bf(u['k'])] * N_HEADS, axis=0)
        v4 = jnp.concatenate([u['v']] * N_HEADS, axis=0)
        u['k_exp'] = jnp.where(same_head, k4, jnp.zeros_like(k4))
        u['v_exp'] = jnp.where(same_head, v4, jnp.zeros_like(v4))
    for u in blocks:
        u['log_w'] = jnp.where(u['causal'], u['bb'] + u['src_row'], -jnp.inf)
        u['qk'] = _dot_nt(u['q'], u['k_exp'])
        u['gl'] = u['b_last'] - u['bb'] + u['li']
    for u in blocks:
        cols = []
        for half in range(2):
            x = u['log_w'][:, half * 128:(half + 1) * 128]
            lo = jnp.max(jnp.where(lane128 < HEAD_DIM, x, -jnp.inf), axis=1, keepdims=True)
            hi = jnp.max(jnp.where(lane128 >= HEAD_DIM, x, -jnp.inf), axis=1, keepdims=True)
            cols.append(jnp.where(lane128 < HEAD_DIM, lo, hi))
        u['a'] = jnp.concatenate(cols, axis=1)
    for u in blocks:
        last = u['last']
        u['a_last'] = u['a'][last:last + 1, :]
        u['s_loc'] = u['qk'] * jnp.exp(u['log_w'] - u['a'])
    for u in blocks:
        u['wk'] = jnp.exp(u['gl'] - u['a_last']) * u['k']
        u['den_loc'] = _head_sum(u['s_loc'], ones)
        u['num_loc'] = _dot(_bf(u['s_loc']), u['v_exp'])
    for u in blocks:
        u['c_loc'] = jnp.where(same_head, _dot_tn(u['v'], _bf(u['wk'])), 0.0)
        u['n_loc'] = jnp.sum(u['wk'], axis=0, keepdims=True)

    cs = [c_ref[d] for d in range(N_DIR)]
    ns = [n_ref[d:d + 1, :] for d in range(N_DIR)]
    ms = [m_ref[d:d + 1, :] for d in range(N_DIR)]
    for c in range(cb):
        now = [u for u in blocks if u['c'] == c]
        for u in now:
            d = u['d']
            u['c_in'], u['n_in'], u['m_in'] = cs[d], ns[d], ms[d]
            u['m_new'] = jnp.maximum(u['b_last'] + ms[d], u['a_last'])
        for u in now:
            u['carry'] = jnp.exp(u['b_last'] + u['m_in'] - u['m_new'])
            u['fresh'] = jnp.exp(u['a_last'] - u['m_new'])
        for u in now:
            d = u['d']
            cs[d] = u['carry'] * cs[d] + u['fresh'] * u['c_loc']
            ns[d] = u['carry'] * ns[d] + u['fresh'] * u['n_loc']
            ms[d] = u['m_new']

    for u in blocks:
        u['qc'] = _dot_nt(u['q'], _bf(u['c_in']))
        u['qn'] = _head_sum(u['q'].astype(F32) * u['n_in'], ones)
        u['m_t'] = jnp.maximum(u['bb'] + u['m_in'], u['a'])
    for u in blocks:
        u['scale'] = jnp.exp(u['a'] - u['m_t'])
        u['inter'] = jnp.exp(u['bb'] + u['m_in'] - u['m_t'])
        u['floor'] = jnp.exp(-u['m_t'])
    for u in blocks:
        u['num'] = u['scale'] * u['num_loc'] + u['inter'] * u['qc']
        u['den'] = jnp.maximum(jnp.abs(u['scale'] * u['den_loc'] + u['inter'] * u['qn']), u['floor'])
    for u in blocks:
        h_ref = h1_ref if u['reverse'] else h0_ref
        h_ref[_chunk_rows(u['c'], cb, u['reverse']), :] = u['num'] / u['den']
    for d in range(N_DIR):
        c_ref[d] = cs[d]
        n_ref[d:d + 1, :] = ns[d]
        m_ref[d:d + 1, :] = ms[d]

    @pl.when(i == pl.num_programs(1) - 1)
    def _():
        for d in range(N_DIR):
            for h in range(N_HEADS):
                hs = slice(h * HEAD_DIM, (h + 1) * HEAD_DIM)
                cfin_ref[0, d, h] = c_ref[d, hs, hs]
        nfin_ref[0] = n_ref[...]
        mfin_ref[0] = m_ref[...]


def mlstm_scan(q, k, z, g, init, bsz, t):
    cb = SCAN_CB
    blk = cb * CHUNK
    assert t % blk == 0
    nc = t // blk
    vcol = (ZD_OFF + 2 * D_GROUP - ZREST_OFF) // D_GROUP

    def specs(row):
        return [pl.BlockSpec((blk, D_GROUP), lambda bi, i: (row(bi, i), 0)),
                pl.BlockSpec((blk, D_GROUP), lambda bi, i: (row(bi, i), 0)),
                pl.BlockSpec((blk, D_GROUP), lambda bi, i: (row(bi, i), vcol)),
                pl.BlockSpec((blk, ZG_W), lambda bi, i: (row(bi, i), 0))]

    fwd = specs(lambda bi, i: bi * nc + i)
    bwd = specs(lambda bi, i: bi * nc + nc - 1 - i)
    cst = pl.BlockSpec((1, N_DIR, N_HEADS, HEAD_DIM, HEAD_DIM), lambda bi, i: (bi, 0, 0, 0, 0))
    vst = pl.BlockSpec((1, N_DIR, D_GROUP), lambda bi, i: (bi, 0, 0))
    has_init = init is not None
    args = (list(init) if has_init else []) + [q, k, z, g] * 2
    tok = jax.ShapeDtypeStruct((bsz * t, D_GROUP), F32)
    vec = jax.ShapeDtypeStruct((bsz, N_DIR, D_GROUP), F32)
    return pl.pallas_call(
        functools.partial(_mlstm_scan_kernel, has_init, cb),
        grid=(bsz, nc),
        in_specs=([cst, vst, vst] if has_init else []) + fwd + bwd,
        out_specs=[fwd[0], bwd[0], cst, vst, vst],
        out_shape=[tok, tok, jax.ShapeDtypeStruct((bsz, N_DIR, N_HEADS, HEAD_DIM, HEAD_DIM), F32), vec, vec],
        scratch_shapes=[pltpu.VMEM((N_DIR, D_GROUP, D_GROUP), F32),
                        pltpu.VMEM((N_DIR, D_GROUP), F32),
                        pltpu.VMEM((N_DIR, D_GROUP), F32)],
        compiler_params=_params("parallel", "arbitrary"),
        name="mlstm_scan",
    )(*args)


def _mix_out(x_ref, mod_ref, y0_ref, y1_ref, bon_ref, g_ref, lng_ref, lnb_ref, pd_ref, pw_ref, ps_ref,
             cv_ref, clg_ref, clb_ref, cpw_ref, h0_ref, h1_ref, o_ref, hng_ref, w_ref):
    ones = _head_ones()
    inv = 1.0 / HEAD_DIM
    y = y0_ref[...] + y1_ref[...] + bon_ref[...]
    mu = _head_sum(y, ones) * inv
    yc = y - mu
    var = _head_sum(yc * yc, ones) * inv
    ya = (yc * lax.rsqrt(var + RWKV_LN_EPS) * lng_ref[...] + lnb_ref[...]) * g_ref[...]
    y_pool = _mm(pd_ref[...], pw_ref[...]) * ps_ref[...]
    cv = cv_ref[...]
    cmu = jnp.mean(cv, axis=-1, keepdims=True)
    cvar = jnp.mean(jnp.square(cv - cmu), axis=-1, keepdims=True)
    cn = (cv - cmu) * lax.rsqrt(cvar + CONV_LN_EPS) * clg_ref[...] + clb_ref[...]
    y_conv = _mm(cn * _sigmoid(cn), cpw_ref[...])
    hs = (h0_ref[...] + h1_ref[...]) * _sigmoid(o_ref[...])
    yd = hs * lax.rsqrt(_head_sum(hs * hs, ones) * inv + EPS) * hng_ref[...]
    g = D_GROUP
    mixed = (_mm(ya, w_ref[0:g, :]) + _mm(y_pool, w_ref[g:2 * g, :])
             + _mm(y_conv, w_ref[2 * g:3 * g, :]) + _mm(yd, w_ref[3 * g:4 * g, :]))
    return x_ref[...] + mod_ref[0, 2:3, :] * mixed


def _mlp(final, x, mod_ref, g_ref, w1_ref, b1_ref, w2_ref, b2_ref, fg_ref):
    h = _rms(x, g_ref[...]) * (1.0 + mod_ref[0, 4:5, :]) + mod_ref[0, 3:4, :]
    a = jnp.maximum(jnp.dot(h.astype(BF16), w1_ref[...], preferred_element_type=F32) + b1_ref[...], 0.0)
    f = jnp.dot((a * a).astype(BF16), w2_ref[...], preferred_element_type=F32) + b2_ref[...]
    x = x + mod_ref[0, 5:6, :] * f
    return _rms(x, fg_ref[...]) if final else x


def _mix_mlp_kernel(final, *refs):
    mix_refs, mlp_refs, out_ref = refs[:20], refs[20:26], refs[26]
    x = _mix_out(*mix_refs)
    out_ref[...] = _mlp(final, x, mix_refs[1], *mlp_refs)


def mix_mlp(x, mod, ra, pooled, conv, md, z, p, final_g, t, final):
    n = x.shape[0]
    tok = pl.BlockSpec((ROW_TILE, D_GROUP), lambda i: (i, 0))
    ocol = (ZD_OFF + 3 * D_GROUP - ZREST_OFF) // D_GROUP
    vec = _const((1, D_GROUP))
    wide = _const((1, D_MODEL))
    sq = _const((D_GROUP, D_GROUP))
    row = pl.BlockSpec((ROW_TILE, D_MODEL), lambda i: (i, 0))
    return pl.pallas_call(
        functools.partial(_mix_mlp_kernel, final),
        grid=(n // ROW_TILE,),
        in_specs=[row, _mod_spec(mod.shape[0] > 1, t), tok, tok, tok, tok, vec, vec, tok, sq, vec,
                  tok, vec, vec, sq, tok, tok,
                  pl.BlockSpec((ROW_TILE, D_GROUP), lambda i: (i, ocol)), vec, _const((D_MODEL, D_MODEL)),
                  wide, _const((D_MODEL, D_FF)), _const((1, D_FF)), _const((D_FF, D_MODEL)), wide, wide],
        out_specs=row,
        out_shape=jax.ShapeDtypeStruct((n, D_MODEL), F32),
        compiler_params=_params("parallel"),
        name="mix_mlp",
    )(x, mod, ra['y0'], ra['y1'], ra['bon'], ra['g'], p['rwkv_ln_g'], p['rwkv_ln_b'],
      pooled, p['pool_w'], p['pool_scale'], conv, p['conv']['ln_g'], p['conv']['ln_b'], p['conv']['pw'],
      md['h0'], md['h1'], z, p['hn_g'], p['w_out'],
      p['norm2_g'], p['w1'], p['b1'], p['w2'], p['b2'], final_g)


def _trunk_layer(x, mod, p, states, bsz, t, grid, final_g, final):
    z, r, kk, v, g, bon, lw0, lw1, kd0, kd1, b0, b1, q, k, gates = in_proj(x, mod, p, t)
    y0, y1, s_rwkv = rwkv_scan(r, kk, v, (lw0, lw1), (kd0, kd1), (b0, b1),
                               None if states is None else states[0], bsz, t)
    yb = pool_mix(z, bsz, t, grid)
    yc = conv_mix(z, p['conv'], bsz, t)
    h0, h1, s_c, s_n, s_m = mlstm_scan(q, k, z, gates, None if states is None else states[1:], bsz, t)
    x = mix_mlp(x, mod, dict(y0=y0, y1=y1, bon=bon, g=g), yb, yc, dict(h0=h0, h1=h1), z, p, final_g, t, final)
    return x, (s_rwkv, s_c, s_n, s_m)


def _layer_params(l, norm1_g, norm2_g, w_in, w_out, rwkv_mu, rwkv_w0, rwkv_w_up, rwkv_a0, rwkv_a_up, rwkv_g_up,
                  rwkv_k_k, rwkv_k_a, rwkv_r_k, rwkv_ln_g, rwkv_ln_b, pool_w, pool_scale, conv_dw, conv_b,
                  conv_ln_g, conv_ln_b, conv_pw, mlstm_qk_conv, mlstm_i_bias, mlstm_f_bias, mlstm_hn_g,
                  mlp_w1, mlp_b1, mlp_w2, mlp_b2):
    row = lambda a: a[l].reshape(1, -1)
    w = w_in[l]
    a_w, b_w, c_w = 1024, 256, 512
    wa, wb, wc, wd = w[:, :a_w], w[:, a_w:a_w + b_w], w[:, a_w + b_w:a_w + b_w + c_w], w[:, a_w + b_w + c_w:]
    nh = N_DIR * N_HEADS
    qkv, gates, o = wd[:, :3 * D_GROUP], wd[:, 3 * D_GROUP:3 * D_GROUP + 2 * nh], wd[:, 3 * D_GROUP + 2 * nh:]
    w_cat = jnp.concatenate([wa, qkv, o, wc, wb, gates, jnp.zeros((D_MODEL, ZG_W - 2 * nh), F32)], axis=1)
    rank = rwkv_w_up.shape[2]
    zeros = jnp.zeros((N_DIR, rank, D_GROUP), F32)
    pool_bd = jax.scipy.linalg.block_diag(*[pool_w[l, g] for g in range(len(POOL_HALF))])
    gate_bias = jnp.concatenate([mlstm_i_bias[l], mlstm_f_bias[l], jnp.zeros((ZG_W - 2 * nh,), F32)]).reshape(1, ZG_W)
    return dict(
        norm1_g=row(norm1_g), norm2_g=row(norm2_g), w_in=w_cat.astype(BF16), w_out=w_out[l].astype(BF16),
        rwkv=dict(mu=row(rwkv_mu), k_k=row(rwkv_k_k), k_a=row(rwkv_k_a), r_k=row(rwkv_r_k),
                  w0=rwkv_w0[l].reshape(N_DIR, 1, D_GROUP), a0=rwkv_a0[l].reshape(N_DIR, 1, D_GROUP),
                  w_up=jnp.concatenate([rwkv_w_up[l], zeros], axis=1).astype(BF16),
                  a_up=jnp.concatenate([zeros, rwkv_a_up[l]], axis=1).astype(BF16),
                  g_up=rwkv_g_up[l].astype(BF16)),
        rwkv_ln_g=row(rwkv_ln_g), rwkv_ln_b=row(rwkv_ln_b),
        pool_w=pool_bd.astype(BF16), pool_scale=row(pool_scale),
        conv=dict(dw=conv_dw[l], b=row(conv_b), ln_g=row(conv_ln_g), ln_b=row(conv_ln_b), pw=conv_pw[l].astype(BF16)),
        mlstm=dict(qk_conv=mlstm_qk_conv[l], gate_bias=gate_bias),
        hn_g=row(mlstm_hn_g),
        w1=mlp_w1[l].astype(BF16), b1=row(mlp_b1), w2=mlp_w2[l].astype(BF16), b2=row(mlp_b2),
    )


def kernel(x_prompt, x_sample, c, state_rwkv, state_mlstm_C, state_mlstm_n, state_mlstm_m, c_ctx, w_mod, b_mod, norm1_g, norm2_g, w_in, w_out, rwkv_mu, rwkv_w0, rwkv_w_up, rwkv_a0, rwkv_a_up, rwkv_g_up, rwkv_k_k, rwkv_k_a, rwkv_r_k, rwkv_ln_g, rwkv_ln_b, pool_w, pool_scale, conv_dw, conv_b, conv_ln_g, conv_ln_b, conv_pw, mlstm_qk_conv, mlstm_i_bias, mlstm_f_bias, mlstm_hn_g, mlp_w1, mlp_b1, mlp_w2, mlp_b2, final_g):
    bp, tp, d = x_prompt.shape
    bs, ts, _ = x_sample.shape
    nh = N_DIR * N_HEADS
    cvec = jnp.concatenate([c_ctx[None, :], c, jnp.zeros((SUB - 1 - bs, d), F32)], axis=0)
    mods = modulation(cvec, w_mod, b_mod).reshape(DEPTH, SUB, 6, d)
    xp = x_prompt.reshape(bp * tp, d)
    xs = x_sample.reshape(bs * ts, d)
    fg = final_g.reshape(1, d)
    new = [[], [], [], []]
    for l in range(DEPTH):
        p = _layer_params(l, norm1_g, norm2_g, w_in, w_out, rwkv_mu, rwkv_w0, rwkv_w_up, rwkv_a0, rwkv_a_up,
                          rwkv_g_up, rwkv_k_k, rwkv_k_a, rwkv_r_k, rwkv_ln_g, rwkv_ln_b, pool_w, pool_scale,
                          conv_dw, conv_b, conv_ln_g, conv_ln_b, conv_pw, mlstm_qk_conv, mlstm_i_bias,
                          mlstm_f_bias, mlstm_hn_g, mlp_w1, mlp_b1, mlp_w2, mlp_b2)
        final = l == DEPTH - 1
        xp, st = _trunk_layer(xp, mods[l, 0:1], p, None, bp, tp, False, fg, final)
        for acc, s in zip(new, st):
            acc.append(s)
        cached = (state_rwkv[:, l], state_mlstm_C[:, l],
                  state_mlstm_n[:, l].reshape(bs, N_DIR, D_GROUP),
                  jnp.repeat(state_mlstm_m[:, l], HEAD_DIM, axis=-1))
        xs, _ = _trunk_layer(xs, mods[l, 1:1 + bs], p, cached, bs, ts, True, fg, final)
    return (xp.reshape(bp, tp, d), xs.reshape(bs, ts, d),
            jnp.stack(new[0], axis=1), jnp.stack(new[1], axis=1),
            jnp.stack(new[2], axis=1).reshape(bp, DEPTH, N_DIR, N_HEADS, HEAD_DIM),
            jnp.stack(new[3], axis=1).reshape(bp, DEPTH, N_DIR, N_HEADS, HEAD_DIM)[..., 0])
```

```python
import functools
import math

import jax
import jax.numpy as jnp
from jax import lax
from jax.experimental import pallas as pl
from jax.experimental.pallas import tpu as pltpu

F32 = jnp.float32
BF16 = jnp.bfloat16

D_MODEL = 1024
D_GROUP = 256
HEAD_DIM = 64
N_HEADS = 4
N_DIR = 2
DEPTH = 4
GRID_W = 64
POOL_HALF = (1, 2, 4, 8)
MAX_HALF = POOL_HALF[-1]
CONV_W = 31
CONV_PAD = 16
D_FF = 4 * D_MODEL
EPS = 1e-6
RWKV_LN_EPS = 64e-5
CONV_LN_EPS = 1e-5
CHUNK = 64
ROW_TILE = 512
SUB = 8
VMEM_LIMIT = 56 * 1024 * 1024

Z_W = 2944
ZA_W, ZD_QK_W, ZC_W, ZB_W, ZG_W = 1024, 512, 512, 256, 128
ZD_OFF, ZC_OFF, ZB_OFF, ZG_OFF = 1024, 2048, 2560, 2816
ZREST_OFF, ZREST_W = 1536, 1280


def _mm(a, b):
    return jnp.dot(a.astype(BF16), b.astype(BF16), preferred_element_type=F32)


def _split(x, parts):
    out = []
    for _ in range(parts):
        p = x.astype(BF16)
        out.append(p)
        x = x - p.astype(F32)
    return out


def _head_sum(x, ones):
    return sum(jnp.dot(p, ones, preferred_element_type=F32) for p in _split(x, 2))


def _cumsum_mm(tri, x):
    return sum(jnp.dot(tri, p, preferred_element_type=F32) for p in _split(x, 3))


def _tri(n, reverse, strict):
    t = lax.broadcasted_iota(jnp.int32, (n, n), 0)
    s = lax.broadcasted_iota(jnp.int32, (n, n), 1)
    if reverse:
        return (s > t) if strict else (s >= t)
    return (s < t) if strict else (s <= t)


def _head_ones():
    a = lax.broadcasted_iota(jnp.int32, (D_GROUP, D_GROUP), 0) // HEAD_DIM
    b = lax.broadcasted_iota(jnp.int32, (D_GROUP, D_GROUP), 1) // HEAD_DIM
    return (a == b).astype(BF16)


def _sigmoid(x):
    return jax.nn.sigmoid(x)


def _softplus(x):
    return jnp.maximum(x, 0.0) + jnp.log1p(jnp.exp(-jnp.abs(x)))


def _rms(x, g):
    return x * lax.rsqrt(jnp.mean(x * x, axis=-1, keepdims=True) + EPS) * g


def _params(*sem):
    return pltpu.CompilerParams(dimension_semantics=sem, vmem_limit_bytes=VMEM_LIMIT)


def _const(shape):
    return pl.BlockSpec(shape, lambda *_: (0,) * len(shape), pipeline_mode=pl.Buffered(1))


def _halo_specs(width, col, n_rows, tt):
    per = tt // SUB
    last = n_rows // SUB - 1
    prev = pl.BlockSpec((SUB, width), lambda i: (jnp.maximum(i * per - 1, 0), col))
    nxt = pl.BlockSpec((SUB, width), lambda i: (jnp.minimum((i + 1) * per, last), col))
    return prev, nxt


def _mod_kernel(c_ref, w_ref, b_ref, o_ref):
    c = c_ref[...]
    o_ref[0] = _mm(c * _sigmoid(c), w_ref[0]) + b_ref[0]


def modulation(cvec, w_mod, b_mod):
    tn = 1536
    n = w_mod.shape[-1]
    return pl.pallas_call(
        _mod_kernel,
        grid=(DEPTH, n // tn),
        in_specs=[pl.BlockSpec((SUB, D_MODEL), lambda l, j: (0, 0)),
                  pl.BlockSpec((1, D_MODEL, tn), lambda l, j: (l, 0, j)),
                  pl.BlockSpec((1, 1, tn), lambda l, j: (l, 0, j))],
        out_specs=pl.BlockSpec((1, SUB, tn), lambda l, j: (l, 0, j)),
        out_shape=jax.ShapeDtypeStruct((DEPTH, SUB, n), F32),
        compiler_params=_params("parallel", "parallel"),
        name="modulation",
    )(cvec, w_mod, b_mod.reshape(DEPTH, 1, n))


def _mod_spec(per_batch, rows_per_seq):
    if per_batch:
        return pl.BlockSpec((1, 6, D_MODEL), lambda i: (i * ROW_TILE // rows_per_seq, 0, 0))
    return pl.BlockSpec((1, 6, D_MODEL), lambda i: (0, 0, 0))


def _shifted(ze, i, t):
    prev = ze[SUB - 1:SUB - 1 + ROW_TILE, :]
    nxt = ze[SUB + 1:SUB + 1 + ROW_TILE, :]
    pos = (i * ROW_TILE + lax.broadcasted_iota(jnp.int32, (ROW_TILE, 1), 0)) % t
    return jnp.where(pos == 0, 0.0, prev), jnp.where(pos == t - 1, 0.0, nxt)


def _in_proj_kernel(t, x_ref, xp_ref, xn_ref, mod_ref, g_ref, w_ref,
                    mu_ref, kk_ref, ka_ref, rk_ref, w0_ref, a0_ref, wup_ref, aup_ref, gup_ref, cw_ref, gb_ref,
                    z_o, r_o, kk_o, v_o, g_o, bon_o, lw0_o, lw1_o, kd0_o, kd1_o, b0_o, b1_o, q_o, k_o, gate_o):
    i = pl.program_id(0)
    xe = jnp.concatenate([xp_ref[...], x_ref[...], xn_ref[...]], axis=0)
    h = _rms(xe, g_ref[...]) * (1.0 + mod_ref[0, 1:2, :]) + mod_ref[0, 0:1, :]
    ze = jnp.dot(h.astype(BF16), w_ref[...], preferred_element_type=F32)
    mid = slice(SUB, SUB + ROW_TILE)
    z_o[...] = ze[mid, ZREST_OFF:ZREST_OFF + ZREST_W]

    za = ze[:, 0:ZA_W]
    z = za[mid, :]
    prev, nxt = _shifted(za, i, t)
    zm = z + (0.5 * (prev + nxt) - z) * mu_ref[...]
    r, k, v = zm[:, 0:256], zm[:, 256:512], zm[:, 512:768]
    lora = zm[:, 768:896]
    gd = zm[:, 896:1024]
    ones = _head_ones()
    kk = k * kk_ref[...]
    kk = kk * lax.rsqrt(_head_sum(kk * kk, ones) + 1e-12)
    wt = jnp.tanh(lora)
    r_o[...] = r
    kk_o[...] = kk
    v_o[...] = v
    g_o[...] = _mm(_sigmoid(gd), gup_ref[...])
    bon = jnp.zeros_like(r)
    for d, (lw_o, kd_o, b_o) in enumerate(((lw0_o, kd0_o, b0_o), (lw1_o, kd1_o, b1_o))):
        lw_o[...] = -math.exp(-0.5) * _sigmoid(w0_ref[d] + _mm(wt, wup_ref[d]))
        a = _sigmoid(a0_ref[d] + _mm(lora, aup_ref[d]))
        kd = k * (1.0 + (a - 1.0) * ka_ref[...])
        kd_o[...] = kd
        b_o[...] = -a * kk
        bon = bon + _head_sum(r * kd * rk_ref[...], ones) * v
    bon_o[...] = bon

    zq = ze[:, ZD_OFF:ZD_OFF + ZD_QK_W]
    prev, nxt = _shifted(zq, i, t)
    qk = cw_ref[0:1, :] * prev + cw_ref[1:2, :] * zq[mid, :] + cw_ref[2:3, :] * nxt
    qk = qk * _sigmoid(qk)
    q_o[...] = qk[:, :D_GROUP]
    k_o[...] = qk[:, D_GROUP:] * (1.0 / math.sqrt(HEAD_DIM))
    gb = ze[mid, ZG_OFF:ZG_OFF + ZG_W] + gb_ref[...]
    lane = lax.broadcasted_iota(jnp.int32, gb.shape, 1)
    gate_o[...] = jnp.where(lane < N_DIR * N_HEADS, gb, -_softplus(-gb))


def in_proj(x, mod, p, t):
    n = x.shape[0]
    tile = pl.BlockSpec((ROW_TILE, D_MODEL), lambda i: (i, 0))
    prev, nxt = _halo_specs(D_MODEL, 0, n, ROW_TILE)
    out = pl.BlockSpec((ROW_TILE, D_GROUP), lambda i: (i, 0))
    vec = _const((1, D_GROUP))
    rw, ml = p['rwkv'], p['mlstm']
    tok = jax.ShapeDtypeStruct((n, D_GROUP), F32)
    return pl.pallas_call(
        functools.partial(_in_proj_kernel, t),
        grid=(n // ROW_TILE,),
        in_specs=[tile, prev, nxt, _mod_spec(mod.shape[0] > 1, t), _const((1, D_MODEL)), _const((D_MODEL, Z_W)),
                  _const((1, ZA_W)), vec, vec, vec,
                  _const((N_DIR, 1, D_GROUP)), _const((N_DIR, 1, D_GROUP)),
                  _const((N_DIR, 128, D_GROUP)), _const((N_DIR, 128, D_GROUP)), _const((128, D_GROUP)),
                  _const((3, ZD_QK_W)), _const((1, ZG_W))],
        out_specs=[pl.BlockSpec((ROW_TILE, ZREST_W), lambda i: (i, 0))] + [out] * 13
                  + [pl.BlockSpec((ROW_TILE, ZG_W), lambda i: (i, 0))],
        out_shape=[jax.ShapeDtypeStruct((n, ZREST_W), F32)] + [tok] * 13 + [jax.ShapeDtypeStruct((n, ZG_W), F32)],
        compiler_params=_params("parallel"),
        name="in_proj",
    )(x, x, x, mod, p['norm1_g'], p['w_in'], rw['mu'], rw['k_k'], rw['k_a'], rw['r_k'], rw['w0'], rw['a0'],
      rw['w_up'], rw['a_up'], rw['g_up'], ml['qk_conv'], ml['gate_bias'])


SCAN_CB = 4


def _bf(x):
    return x.astype(BF16)


def _dot(a, b):
    return jnp.dot(a, b, preferred_element_type=F32)


def _dot_nt(a, b):
    return lax.dot_general(a, b, (((1,), (1,)), ((), ())), preferred_element_type=F32)


def _dot_tn(a, b):
    return lax.dot_general(a, b, (((0,), (0,)), ((), ())), preferred_element_type=F32)


def _tri2(reverse, strict):
    t = lax.broadcasted_iota(jnp.int32, (CHUNK, 2 * CHUNK), 0)
    s = lax.broadcasted_iota(jnp.int32, (CHUNK, 2 * CHUNK), 1) % CHUNK
    if reverse:
        return (s > t) if strict else (s >= t)
    return (s < t) if strict else (s <= t)


def _chunk_rows(c, cb, reverse):
    return slice((cb - 1 - c) * CHUNK, (cb - c) * CHUNK) if reverse else slice(c * CHUNK, (c + 1) * CHUNK)


def _rwkv_scan_kernel(has_init, cb, *refs):
    if has_init:
        s0_ref, refs = refs[0], refs[1:]
    ins, (y0_ref, y1_ref, sfin_ref, s_ref) = refs[:12], refs[12:]
    L = CHUNK
    i = pl.program_id(1)

    @pl.when(i == 0)
    def _():
        if has_init:
            s_ref[...] = s0_ref[0]
        else:
            s_ref[...] = jnp.zeros_like(s_ref)

    eye = (lax.broadcasted_iota(jnp.int32, (L, L), 0) == lax.broadcasted_iota(jnp.int32, (L, L), 1)).astype(F32)
    heads = [slice(h * HEAD_DIM, (h + 1) * HEAD_DIM) for h in range(N_HEADS)]

    blocks = []
    for d in range(N_DIR):
        reverse = d == 1
        incl = _tri(L, reverse, False)
        for c in range(cb):
            rows = _chunk_rows(c, cb, reverse)
            blocks.append(dict(d=d, c=c, rows=rows, reverse=reverse, incl=incl, lw=ins[6 * d + 3][rows, :]))
    for u in blocks:
        u['cum'] = _cumsum_mm(u['incl'].astype(BF16), u['lw'])
    inst = []
    for u in blocks:
        d, rows, cum, lw = u['d'], u['rows'], u['cum'], u['lw']
        r_ref, kk_ref, v_ref, _, kd_ref, b_ref = ins[6 * d:6 * d + 6]
        r, kk, v, kd, b = (ref[rows, :] for ref in (r_ref, kk_ref, v_ref, kd_ref, b_ref))
        last = cum[0:1, :] if u['reverse'] else cum[L - 1:L, :]
        e_neg = jnp.exp(-cum)
        e_end = jnp.exp(last - cum)
        e_all = jnp.exp(last)
        aq = _bf(kk * jnp.exp(cum - lw))
        rq = r * jnp.exp(cum)
        bk = _bf(b * e_neg)
        kq = _bf(kd * e_neg)
        k_end = _bf(kd * e_end)
        b_end = _bf(b * e_end)
        rq_b, v_b = _bf(rq), _bf(v)
        for h, hs in enumerate(heads):
            inst.append(dict(d=d, c=u['c'], h=h, reverse=u['reverse'], aq=aq[:, hs],
                             rq=rq[:, hs], rq_b=rq_b[:, hs], bk=bk[:, hs], kq=kq[:, hs], k_end=k_end[:, hs],
                             b_end=b_end[:, hs], v=v_b[:, hs], e_all=e_all[:, hs]))

    zeros_h = jnp.zeros((L, HEAD_DIM), BF16)
    for t in inst:
        res = _dot_nt(jnp.concatenate([t['aq'], t['rq_b']], axis=0), jnp.concatenate([t['bk'], t['kq']], axis=0))
        t['mn'] = jnp.where(_tri2(t['reverse'], True), res[:L], 0.0)
        t['pp'] = _bf(jnp.where(_tri2(t['reverse'], False), res[L:], 0.0))
        t['m'] = t['mn'][:, :L]
    row = lax.broadcasted_iota(jnp.int32, (L, L), 0)
    col = lax.broadcasted_iota(jnp.int32, (L, L), 1)
    sizes = [2 ** e for e in range(int(math.log2(L)))]
    corner = {s: jnp.logical_and(row // (2 * s) == col // (2 * s), row // s != col // s) for s in sizes}
    for t in inst:
        t['y'] = eye + jnp.where(corner[1], t['m'], 0.0)
    for s in sizes[1:]:
        for t in inst:
            t['my'] = _bf(_dot(_bf(jnp.where(corner[s], t['m'], 0.0)), _bf(t['y'])))
        for t in inst:
            t['y'] = t['y'] + _dot(_bf(t['y']), t['my'])
    for t in inst:
        t['y'] = _bf(t['y'])
        t['nv'] = _bf(_dot(_bf(t['mn']), jnp.concatenate([zeros_h, t['v']], axis=0)))
    for t in inst:
        t['wq'] = _bf(_dot(t['y'], jnp.concatenate([t['nv'], t['aq']], axis=1)))
    for t in inst:
        x = _dot(t['pp'], jnp.concatenate([t['wq'], jnp.concatenate([t['v'], zeros_h], axis=1)], axis=0))
        t['y_loc'] = x[:, :L]
        t['r2'] = _bf(t['rq'] + x[:, L:])
        wb = _dot_tn(jnp.concatenate([t['wq'], t['v']], axis=1), jnp.concatenate([t['b_end'], t['k_end']], axis=1))
        t['tq'] = _bf(wb[L:2 * L, :L])
        t['g'] = wb[:L, :L] + wb[2 * L:, L:]

    state = [[s_ref[d, h] for h in range(N_HEADS)] for d in range(N_DIR)]
    outs = {}
    for c in range(cb):
        now = [t for t in inst if t['c'] == c]
        for t in now:
            s0 = state[t['d']][t['h']]
            s0_b = _bf(s0)
            outs[(t['d'], c, t['h'])] = t['y_loc'] + _dot_nt(t['r2'], s0_b)
            t['s1'] = s0 * t['e_all'] + _dot(s0_b, t['tq']) + t['g']
        for t in now:
            state[t['d']][t['h']] = t['s1']
    for d, y_ref in enumerate((y0_ref, y1_ref)):
        for c in range(cb):
            y_ref[_chunk_rows(c, cb, d == 1), :] = jnp.concatenate([outs[(d, c, h)] for h in range(N_HEADS)], axis=1)
    for d in range(N_DIR):
        for h in range(N_HEADS):
            s_ref[d, h] = state[d][h]

    @pl.when(i == pl.num_programs(1) - 1)
    def _():
        sfin_ref[0] = s_ref[...]


def rwkv_scan(r, kk, v, lw, kd, b, s0, bsz, t):
    cb = SCAN_CB
    blk = cb * CHUNK
    assert t % blk == 0
    nc = t // blk
    fwd = pl.BlockSpec((blk, D_GROUP), lambda bi, i: (bi * nc + i, 0))
    bwd = pl.BlockSpec((blk, D_GROUP), lambda bi, i: (bi * nc + nc - 1 - i, 0))
    st = pl.BlockSpec((1, N_DIR, N_HEADS, HEAD_DIM, HEAD_DIM), lambda bi, i: (bi, 0, 0, 0, 0))
    has_init = s0 is not None
    args = ([s0] if has_init else []) + [r, kk, v, lw[0], kd[0], b[0], r, kk, v, lw[1], kd[1], b[1]]
    in_specs = ([st] if has_init else []) + [fwd] * 6 + [bwd] * 6
    tok = jax.ShapeDtypeStruct((bsz * t, D_GROUP), F32)
    return pl.pallas_call(
        functools.partial(_rwkv_scan_kernel, has_init, cb),
        grid=(bsz, nc),
        in_specs=in_specs,
        out_specs=[fwd, bwd, st],
        out_shape=[tok, tok, jax.ShapeDtypeStruct((bsz, N_DIR, N_HEADS, HEAD_DIM, HEAD_DIM), F32)],
        scratch_shapes=[pltpu.VMEM((N_DIR, N_HEADS, HEAD_DIM, HEAD_DIM), F32)],
        compiler_params=_params("parallel", "arbitrary"),
        name="rwkv_scan",
    )(*args)


def _shift_rows(x, k, pos, n):
    rows = x.shape[0]
    rolled = pltpu.roll(x, k % rows, axis=0)
    ok = (pos >= k) if k > 0 else (pos < n + k)
    return jnp.where(ok, rolled, 0.0)


def _window_sums(x, pos, n):
    fw = x
    bw = _shift_rows(x, 1, pos, n)
    out = [fw + bw]
    for h in POOL_HALF[:-1]:
        fw = fw + _shift_rows(fw, -h, pos, n)
        bw = bw + _shift_rows(bw, h, pos, n)
        out.append(fw + bw)
    return out


def _by_group(vals, lane):
    out = vals[-1]
    for g in range(len(vals) - 2, -1, -1):
        out = jnp.where(lane < (g + 1) * HEAD_DIM, vals[g], out)
    return out


def _clipped_count(pos, h, n):
    return jnp.minimum(pos + h, n) - jnp.maximum(pos - h, 0)


def _pool_seq_kernel(x_ref, o_ref):
    x = x_ref[...]
    t = x.shape[0]
    pos = lax.broadcasted_iota(jnp.int32, x.shape, 0)
    lane = lax.broadcasted_iota(jnp.int32, x.shape, 1)
    sums = _by_group(_window_sums(x, pos, t), lane)
    cnt = _by_group([_clipped_count(pos, h, t) for h in POOL_HALF], lane).astype(F32)
    o_ref[...] = sums / cnt - x


def _pool_grid_kernel(x_ref, o_ref, cs_ref):
    gw = GRID_W
    rows = x_ref.shape[0] // gw
    shape = (gw, D_GROUP)
    col = lax.broadcasted_iota(jnp.int32, shape, 0)
    lane = lax.broadcasted_iota(jnp.int32, shape, 1)
    half = _by_group([jnp.full(shape, h, jnp.int32) for h in POOL_HALF], lane)
    cnt_c = _by_group([_clipped_count(col, h, gw) for h in POOL_HALF], lane)

    def col_stage(r, _):
        at = pl.ds(pl.multiple_of(r * gw, gw), gw)
        cs_ref[at, :] = _by_group(_window_sums(x_ref[at, :], col, gw), lane)
        return 0

    lax.fori_loop(0, rows, col_stage, 0)

    def row_stage(r, _):
        at = pl.ds(pl.multiple_of(r * gw, gw), gw)
        acc = jnp.zeros(shape, F32)
        for o in range(-MAX_HALF, MAX_HALF):
            rr = r + o
            src = pl.ds(pl.multiple_of(jnp.clip(rr, 0, rows - 1) * gw, gw), gw)
            in_window = (half > o) if o >= 0 else (half >= -o)
            ok = jnp.logical_and(in_window, jnp.logical_and(rr >= 0, rr < rows))
            acc = acc + jnp.where(ok, cs_ref[src, :], 0.0)
        cnt = (_clipped_count(r, half, rows) * cnt_c).astype(F32)
        o_ref[at, :] = acc / cnt - x_ref[at, :]
        return 0

    lax.fori_loop(0, rows, row_stage, 0)


def pool_mix(z, bsz, t, grid):
    n = bsz * t
    col = (ZB_OFF - ZREST_OFF) // ZB_W
    scratch = [pltpu.VMEM((t, D_GROUP), F32)] if grid else []
    return pl.pallas_call(
        _pool_grid_kernel if grid else _pool_seq_kernel,
        grid=(bsz,),
        in_specs=[pl.BlockSpec((t, ZB_W), lambda i: (i, col))],
        out_specs=pl.BlockSpec((t, D_GROUP), lambda i: (i, 0)),
        out_shape=jax.ShapeDtypeStruct((n, D_GROUP), F32),
        scratch_shapes=scratch,
        compiler_params=_params("parallel"),
        name="pool_grid" if grid else "pool_seq",
    )(z)


CONV_TILE = 64


def _conv_kernel(z_ref, dw_ref, b_ref, o_ref, pad_ref):
    t = z_ref.shape[0]
    rt = CONV_TILE
    edge = jnp.zeros((CONV_PAD, D_GROUP), F32)
    pad_ref[0:CONV_PAD, :] = edge
    pad_ref[t + CONV_PAD:t + 2 * CONV_PAD, :] = edge

    def glu(i, _):
        base = pl.multiple_of(i * rt, rt)
        zz = z_ref[pl.ds(base, rt), :]
        pad_ref[pl.ds(base + CONV_PAD, rt), :] = zz[:, :D_GROUP] * _sigmoid(zz[:, D_GROUP:])
        return 0

    lax.fori_loop(0, t // rt, glu, 0)

    def tile(i, _):
        base = pl.multiple_of(i * rt, rt)
        rows = rt + 2 * CONV_PAD
        first = CONV_PAD - CONV_W // 2
        halves = []
        for lanes in (slice(0, D_GROUP // 2), slice(D_GROUP // 2, D_GROUP)):
            acc = jnp.broadcast_to(b_ref[:, lanes], (rt, D_GROUP // 2))
            win = pad_ref[pl.ds(base, rows), lanes]
            for sub in range(SUB):
                shifted = pltpu.roll(win, (rows - sub) % rows, axis=0) if sub else win
                for j in range(CONV_W):
                    off = first + j
                    if off % SUB == sub:
                        acc = acc + dw_ref[j:j + 1, lanes] * shifted[off - sub:off - sub + rt, :]
            halves.append(acc)
        o_ref[pl.ds(base, rt), :] = jnp.concatenate(halves, axis=1)
        return 0

    lax.fori_loop(0, t // rt, tile, 0)


def conv_mix(z, p, bsz, t):
    n = bsz * t
    col = (ZC_OFF - ZREST_OFF) // ZC_W
    vec = _const((1, D_GROUP))
    return pl.pallas_call(
        _conv_kernel,
        grid=(bsz,),
        in_specs=[pl.BlockSpec((t, ZC_W), lambda i: (i, col)), _const((CONV_W, D_GROUP)), vec],
        out_specs=pl.BlockSpec((t, D_GROUP), lambda i: (i, 0)),
        out_shape=jax.ShapeDtypeStruct((n, D_GROUP), F32),
        scratch_shapes=[pltpu.VMEM((t + 2 * CONV_PAD, D_GROUP), F32)],
        compiler_params=_params("parallel"),
        name="conv_mix",
    )(z, p['dw'], p['b'])


def _lane_group(n_lanes):
    return lax.broadcasted_iota(jnp.int32, (1, n_lanes), 1) // HEAD_DIM


def _mlstm_scan_kernel(has_init, cb, *refs):
    if has_init:
        c0_ref, n0_ref, m0_ref = refs[:3]
        refs = refs[3:]
    ins, (h0_ref, h1_ref, cfin_ref, nfin_ref, mfin_ref, c_ref, n_ref, m_ref) = refs[:8], refs[8:]
    L = CHUNK
    nh = N_DIR * N_HEADS
    W = D_GROUP
    i = pl.program_id(1)

    @pl.when(i == 0)
    def _():
        c_ref[...] = jnp.zeros_like(c_ref)
        if has_init:
            for d in range(N_DIR):
                for h in range(N_HEADS):
                    hs = slice(h * HEAD_DIM, (h + 1) * HEAD_DIM)
                    c_ref[d, hs, hs] = c0_ref[0, d, h]
            n_ref[...] = n0_ref[0]
            m_ref[...] = m0_ref[0]
        else:
            n_ref[...] = jnp.zeros_like(n_ref)
            m_ref[...] = jnp.zeros_like(m_ref)

    ones = _head_ones()
    row_head = lax.broadcasted_iota(jnp.int32, (W, 1), 0) // HEAD_DIM
    same_head = row_head == _lane_group(W)
    gate_lane = lax.broadcasted_iota(jnp.int32, (ZG_W, 1), 0)
    pick = (lax.broadcasted_iota(jnp.int32, (nh, ZG_W), 0) == lax.broadcasted_iota(jnp.int32, (nh, ZG_W), 1)).astype(BF16)
    s_pos = lax.broadcasted_iota(jnp.int32, (L, W), 1) % L
    t_pos = lax.broadcasted_iota(jnp.int32, (L, W), 0)
    lane128 = lax.broadcasted_iota(jnp.int32, (L, 128), 1)

    blocks = []
    for d in range(N_DIR):
        reverse = d == 1
        q_ref, k_ref, v_ref, g_ref = ins[4 * d:4 * d + 4]
        spread_i = (gate_lane == d * N_HEADS + _lane_group(W)).astype(BF16)
        spread_b = (gate_lane == nh + d * N_HEADS + _lane_group(W)).astype(BF16)
        for c in range(cb):
            rows = _chunk_rows(c, cb, reverse)
            blocks.append(dict(d=d, c=c, reverse=reverse, last=0 if reverse else L - 1,
                               causal=(s_pos >= t_pos) if reverse else (s_pos <= t_pos),
                               tri=_tri(L, reverse, False).astype(BF16), spread_i=spread_i, spread_b=spread_b,
                               q=_bf(q_ref[rows, :]), k=k_ref[rows, :], v=_bf(v_ref[rows, :]), g=g_ref[rows, :]))
    for u in blocks:
        u['bc_all'] = _cumsum_mm(u['tri'], u['g'])
    for u in blocks:
        u['src'] = u['g'] - pltpu.roll(u['bc_all'], ZG_W - nh, axis=1)
        u['bb'] = sum(_dot(p, u['spread_b']) for p in _split(u['bc_all'], 3))
        u['li'] = sum(_dot(p, u['spread_i']) for p in _split(u['g'], 3))
    for u in blocks:
        rows8 = sum(_dot_nt(pick, p) for p in _split(u['src'], 3))
        j0 = u['d'] * N_HEADS
        u['src_row'] = jnp.concatenate([rows8[j0 + h:j0 + h + 1, :] for h in range(N_HEADS)], axis=1)
        last = u['last']
        u['b_last'] = u['bb'][last:last + 1, :]
        k4 = jnp.concatenate([_bf(u['k'])] * N_HEADS, axis=0)
        v4 = jnp.concatenate([u['v']] * N_HEADS, axis=0)
        u['k_exp'] = jnp.where(same_head, k4, jnp.zeros_like(k4))
        u['v_exp'] = jnp.where(same_head, v4, jnp.zeros_like(v4))
    for u in blocks:
        u['log_w'] = jnp.where(u['causal'], u['bb'] + u['src_row'], -jnp.inf)
        u['qk'] = _dot_nt(u['q'], u['k_exp'])
        u['gl'] = u['b_last'] - u['bb'] + u['li']
    for u in blocks:
        cols = []
        for half in range(2):
            x = u['log_w'][:, half * 128:(half + 1) * 128]
            lo = jnp.max(jnp.where(lane128 < HEAD_DIM, x, -jnp.inf), axis=1, keepdims=True)
            hi = jnp.max(jnp.where(lane128 >= HEAD_DIM, x, -jnp.inf), axis=1, keepdims=True)
            cols.append(jnp.where(lane128 < HEAD_DIM, lo, hi))
        u['a'] = jnp.concatenate(cols, axis=1)
    for u in blocks:
        last = u['last']
        u['a_last'] = u['a'][last:last + 1, :]
        u['s_loc'] = u['qk'] * jnp.exp(u['log_w'] - u['a'])
    for u in blocks:
        u['wk'] = jnp.exp(u['gl'] - u['a_last']) * u['k']
        u['den_loc'] = _head_sum(u['s_loc'], ones)
        u['num_loc'] = _dot(_bf(u['s_loc']), u['v_exp'])
    for u in blocks:
        u['c_loc'] = jnp.where(same_head, _dot_tn(u['v'], _bf(u['wk'])), 0.0)
        u['n_loc'] = jnp.sum(u['wk'], axis=0, keepdims=True)

    cs = [c_ref[d] for d in range(N_DIR)]
    ns = [n_ref[d:d + 1, :] for d in range(N_DIR)]
    ms = [m_ref[d:d + 1, :] for d in range(N_DIR)]
    for c in range(cb):
        now = [u for u in blocks if u['c'] == c]
        for u in now:
            d = u['d']
            u['c_in'], u['n_in'], u['m_in'] = cs[d], ns[d], ms[d]
            u['m_new'] = jnp.maximum(u['b_last'] + ms[d], u['a_last'])
        for u in now:
            u['carry'] = jnp.exp(u['b_last'] + u['m_in'] - u['m_new'])
            u['fresh'] = jnp.exp(u['a_last'] - u['m_new'])
        for u in now:
            d = u['d']
            cs[d] = u['carry'] * cs[d] + u['fresh'] * u['c_loc']
            ns[d] = u['carry'] * ns[d] + u['fresh'] * u['n_loc']
            ms[d] = u['m_new']

    for u in blocks:
        u['qc'] = _dot_nt(u['q'], _bf(u['c_in']))
        u['qn'] = _head_sum(u['q'].astype(F32) * u['n_in'], ones)
        u['m_t'] = jnp.maximum(u['bb'] + u['m_in'], u['a'])
    for u in blocks:
        u['scale'] = jnp.exp(u['a'] - u['m_t'])
        u['inter'] = jnp.exp(u['bb'] + u['m_in'] - u['m_t'])
        u['floor'] = jnp.exp(-u['m_t'])
    for u in blocks:
        u['num'] = u['scale'] * u['num_loc'] + u['inter'] * u['qc']
        u['den'] = jnp.maximum(jnp.abs(u['scale'] * u['den_loc'] + u['inter'] * u['qn']), u['floor'])
    for u in blocks:
        h_ref = h1_ref if u['reverse'] else h0_ref
        h_ref[_chunk_rows(u['c'], cb, u['reverse']), :] = u['num'] / u['den']
    for d in range(N_DIR):
        c_ref[d] = cs[d]
        n_ref[d:d + 1, :] = ns[d]
        m_ref[d:d + 1, :] = ms[d]

    @pl.when(i == pl.num_programs(1) - 1)
    def _():
        for d in range(N_DIR):
            for h in range(N_HEADS):
                hs = slice(h * HEAD_DIM, (h + 1) * HEAD_DIM)
                cfin_ref[0, d, h] = c_ref[d, hs, hs]
        nfin_ref[0] = n_ref[...]
        mfin_ref[0] = m_ref[...]


def mlstm_scan(q, k, z, g, init, bsz, t):
    cb = SCAN_CB
    blk = cb * CHUNK
    assert t % blk == 0
    nc = t // blk
    vcol = (ZD_OFF + 2 * D_GROUP - ZREST_OFF) // D_GROUP

    def specs(row):
        return [pl.BlockSpec((blk, D_GROUP), lambda bi, i: (row(bi, i), 0)),
                pl.BlockSpec((blk, D_GROUP), lambda bi, i: (row(bi, i), 0)),
                pl.BlockSpec((blk, D_GROUP), lambda bi, i: (row(bi, i), vcol)),
                pl.BlockSpec((blk, ZG_W), lambda bi, i: (row(bi, i), 0))]

    fwd = specs(lambda bi, i: bi * nc + i)
    bwd = specs(lambda bi, i: bi * nc + nc - 1 - i)
    cst = pl.BlockSpec((1, N_DIR, N_HEADS, HEAD_DIM, HEAD_DIM), lambda bi, i: (bi, 0, 0, 0, 0))
    vst = pl.BlockSpec((1, N_DIR, D_GROUP), lambda bi, i: (bi, 0, 0))
    has_init = init is not None
    args = (list(init) if has_init else []) + [q, k, z, g] * 2
    tok = jax.ShapeDtypeStruct((bsz * t, D_GROUP), F32)
    vec = jax.ShapeDtypeStruct((bsz, N_DIR, D_GROUP), F32)
    return pl.pallas_call(
        functools.partial(_mlstm_scan_kernel, has_init, cb),
        grid=(bsz, nc),
        in_specs=([cst, vst, vst] if has_init else []) + fwd + bwd,
        out_specs=[fwd[0], bwd[0], cst, vst, vst],
        out_shape=[tok, tok, jax.ShapeDtypeStruct((bsz, N_DIR, N_HEADS, HEAD_DIM, HEAD_DIM), F32), vec, vec],
        scratch_shapes=[pltpu.VMEM((N_DIR, D_GROUP, D_GROUP), F32),
                        pltpu.VMEM((N_DIR, D_GROUP), F32),
                        pltpu.VMEM((N_DIR, D_GROUP), F32)],
        compiler_params=_params("parallel", "arbitrary"),
        name="mlstm_scan",
    )(*args)


def _mix_out(x_ref, mod_ref, y0_ref, y1_ref, bon_ref, g_ref, lng_ref, lnb_ref, pd_ref, pw_ref, ps_ref,
             cv_ref, clg_ref, clb_ref, cpw_ref, h0_ref, h1_ref, o_ref, hng_ref, w_ref):
    ones = _head_ones()
    inv = 1.0 / HEAD_DIM
    y = y0_ref[...] + y1_ref[...] + bon_ref[...]
    mu = _head_sum(y, ones) * inv
    yc = y - mu
    var = _head_sum(yc * yc, ones) * inv
    ya = (yc * lax.rsqrt(var + RWKV_LN_EPS) * lng_ref[...] + lnb_ref[...]) * g_ref[...]
    y_pool = _mm(pd_ref[...], pw_ref[...]) * ps_ref[...]
    cv = cv_ref[...]
    cmu = jnp.mean(cv, axis=-1, keepdims=True)
    cvar = jnp.mean(jnp.square(cv - cmu), axis=-1, keepdims=True)
    cn = (cv - cmu) * lax.rsqrt(cvar + CONV_LN_EPS) * clg_ref[...] + clb_ref[...]
    y_conv = _mm(cn * _sigmoid(cn), cpw_ref[...])
    hs = (h0_ref[...] + h1_ref[...]) * _sigmoid(o_ref[...])
    yd = hs * lax.rsqrt(_head_sum(hs * hs, ones) * inv + EPS) * hng_ref[...]
    g = D_GROUP
    mixed = (_mm(ya, w_ref[0:g, :]) + _mm(y_pool, w_ref[g:2 * g, :])
             + _mm(y_conv, w_ref[2 * g:3 * g, :]) + _mm(yd, w_ref[3 * g:4 * g, :]))
    return x_ref[...] + mod_ref[0, 2:3, :] * mixed


def _mlp(final, x, mod_ref, g_ref, w1_ref, b1_ref, w2_ref, b2_ref, fg_ref):
    h = _rms(x, g_ref[...]) * (1.0 + mod_ref[0, 4:5, :]) + mod_ref[0, 3:4, :]
    a = jnp.maximum(jnp.dot(h.astype(BF16), w1_ref[...], preferred_element_type=F32) + b1_ref[...], 0.0)
    f = jnp.dot((a * a).astype(BF16), w2_ref[...], preferred_element_type=F32) + b2_ref[...]
    x = x + mod_ref[0, 5:6, :] * f
    return _rms(x, fg_ref[...]) if final else x


def _mix_mlp_kernel(final, *refs):
    mix_refs, mlp_refs, out_ref = refs[:20], refs[20:26], refs[26]
    x = _mix_out(*mix_refs)
    out_ref[...] = _mlp(final, x, mix_refs[1], *mlp_refs)


def mix_mlp(x, mod, ra, pooled, conv, md, z, p, final_g, t, final):
    n = x.shape[0]
    tok = pl.BlockSpec((ROW_TILE, D_GROUP), lambda i: (i, 0))
    ocol = (ZD_OFF + 3 * D_GROUP - ZREST_OFF) // D_GROUP
    vec = _const((1, D_GROUP))
    wide = _const((1, D_MODEL))
    sq = _const((D_GROUP, D_GROUP))
    row = pl.BlockSpec((ROW_TILE, D_MODEL), lambda i: (i, 0))
    return pl.pallas_call(
        functools.partial(_mix_mlp_kernel, final),
        grid=(n // ROW_TILE,),
        in_specs=[row, _mod_spec(mod.shape[0] > 1, t), tok, tok, tok, tok, vec, vec, tok, sq, vec,
                  tok, vec, vec, sq, tok, tok,
                  pl.BlockSpec((ROW_TILE, D_GROUP), lambda i: (i, ocol)), vec, _const((D_MODEL, D_MODEL)),
                  wide, _const((D_MODEL, D_FF)), _const((1, D_FF)), _const((D_FF, D_MODEL)), wide, wide],
        out_specs=row,
        out_shape=jax.ShapeDtypeStruct((n, D_MODEL), F32),
        compiler_params=_params("parallel"),
        name="mix_mlp",
    )(x, mod, ra['y0'], ra['y1'], ra['bon'], ra['g'], p['rwkv_ln_g'], p['rwkv_ln_b'],
      pooled, p['pool_w'], p['pool_scale'], conv, p['conv']['ln_g'], p['conv']['ln_b'], p['conv']['pw'],
      md['h0'], md['h1'], z, p['hn_g'], p['w_out'],
      p['norm2_g'], p['w1'], p['b1'], p['w2'], p['b2'], final_g)


def _trunk_layer(x, mod, p, states, bsz, t, grid, final_g, final):
    z, r, kk, v, g, bon, lw0, lw1, kd0, kd1, b0, b1, q, k, gates = in_proj(x, mod, p, t)
    y0, y1, s_rwkv = rwkv_scan(r, kk, v, (lw0, lw1), (kd0, kd1), (b0, b1),
                               None if states is None else states[0], bsz, t)
    yb = pool_mix(z, bsz, t, grid)
    yc = conv_mix(z, p['conv'], bsz, t)
    h0, h1, s_c, s_n, s_m = mlstm_scan(q, k, z, gates, None if states is None else states[1:], bsz, t)
    x = mix_mlp(x, mod, dict(y0=y0, y1=y1, bon=bon, g=g), yb, yc, dict(h0=h0, h1=h1), z, p, final_g, t, final)
    return x, (s_rwkv, s_c, s_n, s_m)


def _layer_params(l, norm1_g, norm2_g, w_in, w_out, rwkv_mu, rwkv_w0, rwkv_w_up, rwkv_a0, rwkv_a_up, rwkv_g_up,
                  rwkv_k_k, rwkv_k_a, rwkv_r_k, rwkv_ln_g, rwkv_ln_b, pool_w, pool_scale, conv_dw, conv_b,
                  conv_ln_g, conv_ln_b, conv_pw, mlstm_qk_conv, mlstm_i_bias, mlstm_f_bias, mlstm_hn_g,
                  mlp_w1, mlp_b1, mlp_w2, mlp_b2):
    row = lambda a: a[l].reshape(1, -1)
    w = w_in[l]
    a_w, b_w, c_w = 1024, 256, 512
    wa, wb, wc, wd = w[:, :a_w], w[:, a_w:a_w + b_w], w[:, a_w + b_w:a_w + b_w + c_w], w[:, a_w + b_w + c_w:]
    nh = N_DIR * N_HEADS
    qkv, gates, o = wd[:, :3 * D_GROUP], wd[:, 3 * D_GROUP:3 * D_GROUP + 2 * nh], wd[:, 3 * D_GROUP + 2 * nh:]
    w_cat = jnp.concatenate([wa, qkv, o, wc, wb, gates, jnp.zeros((D_MODEL, ZG_W - 2 * nh), F32)], axis=1)
    rank = rwkv_w_up.shape[2]
    zeros = jnp.zeros((N_DIR, rank, D_GROUP), F32)
    pool_bd = jax.scipy.linalg.block_diag(*[pool_w[l, g] for g in range(len(POOL_HALF))])
    gate_bias = jnp.concatenate([mlstm_i_bias[l], mlstm_f_bias[l], jnp.zeros((ZG_W - 2 * nh,), F32)]).reshape(1, ZG_W)
    return dict(
        norm1_g=row(norm1_g), norm2_g=row(norm2_g), w_in=w_cat.astype(BF16), w_out=w_out[l].astype(BF16),
        rwkv=dict(mu=row(rwkv_mu), k_k=row(rwkv_k_k), k_a=row(rwkv_k_a), r_k=row(rwkv_r_k),
                  w0=rwkv_w0[l].reshape(N_DIR, 1, D_GROUP), a0=rwkv_a0[l].reshape(N_DIR, 1, D_GROUP),
                  w_up=jnp.concatenate([rwkv_w_up[l], zeros], axis=1).astype(BF16),
                  a_up=jnp.concatenate([zeros, rwkv_a_up[l]], axis=1).astype(BF16),
                  g_up=rwkv_g_up[l].astype(BF16)),
        rwkv_ln_g=row(rwkv_ln_g), rwkv_ln_b=row(rwkv_ln_b),
        pool_w=pool_bd.astype(BF16), pool_scale=row(pool_scale),
        conv=dict(dw=conv_dw[l], b=row(conv_b), ln_g=row(conv_ln_g), ln_b=row(conv_ln_b), pw=conv_pw[l].astype(BF16)),
        mlstm=dict(qk_conv=mlstm_qk_conv[l], gate_bias=gate_bias),
        hn_g=row(mlstm_hn_g),
        w1=mlp_w1[l].astype(BF16), b1=row(mlp_b1), w2=mlp_w2[l].astype(BF16), b2=row(mlp_b2),
    )


def kernel(x_prompt, x_sample, c, state_rwkv, state_mlstm_C, state_mlstm_n, state_mlstm_m, c_ctx, w_mod, b_mod, norm1_g, norm2_g, w_in, w_out, rwkv_mu, rwkv_w0, rwkv_w_up, rwkv_a0, rwkv_a_up, rwkv_g_up, rwkv_k_k, rwkv_k_a, rwkv_r_k, rwkv_ln_g, rwkv_ln_b, pool_w, pool_scale, conv_dw, conv_b, conv_ln_g, conv_ln_b, conv_pw, mlstm_qk_conv, mlstm_i_bias, mlstm_f_bias, mlstm_hn_g, mlp_w1, mlp_b1, mlp_w2, mlp_b2, final_g):
    bp, tp, d = x_prompt.shape
    bs, ts, _ = x_sample.shape
    cvec =jnp.concatenate([c_ctx[None, :], c, jnp.zeros((SUB - 1 - bs, d), F32)], axis=0)
    mods = modulation(cvec, w_mod, b_mod).reshape(DEPTH, SUB, 6, d)
    xp = x_prompt.reshape(bp * tp, d)
    xs = x_sample.reshape(bs * ts, d)
    fg = final_g.reshape(1, d)
    new = [[], [], [], []]
    for l in range(DEPTH):
        p = _layer_params(l, norm1_g, norm2_g, w_in, w_out, rwkv_mu, rwkv_w0, rwkv_w_up, rwkv_a0, rwkv_a_up,
                          rwkv_g_up, rwkv_k_k, rwkv_k_a, rwkv_r_k, rwkv_ln_g, rwkv_ln_b, pool_w, pool_scale,
                          conv_dw, conv_b, conv_ln_g, conv_ln_b, conv_pw, mlstm_qk_conv, mlstm_i_bias,
                          mlstm_f_bias, mlstm_hn_g, mlp_w1, mlp_b1, mlp_w2, mlp_b2)
        final = l == DEPTH - 1
        xp, st = _trunk_layer(xp, mods[l, 0:1], p, None, bp, tp, False, fg, final)
        for acc, s in zip(new, st):
            acc.append(s)
        cached = (state_rwkv[:, l], state_mlstm_C[:, l],
                  state_mlstm_n[:, l].reshape(bs, N_DIR, D_GROUP),
                  jnp.repeat(state_mlstm_m[:, l], HEAD_DIM, axis=-1))
        xs, _ = _trunk_layer(xs, mods[l, 1:1 + bs], p, cached, bs, ts, True, fg, final)
    return (xp.reshape(bp, tp, d), xs.reshape(bs, ts, d),
            jnp.stack(new[0], axis=1), jnp.stack(new[1], axis=1),
            jnp.stack(new[2], axis=1).reshape(bp, DEPTH, N_DIR, N_HEADS, HEAD_DIM),
            jnp.stack(new[3], axis=1).reshape(bp, DEPTH, N_DIR, N_HEADS, HEAD_DIM)[..., 0])
```

```python
import functools
import math

import jax
import jax.numpy as jnp
from jax import lax
from jax.experimental import pallas as pl
from jax.experimental.pallas import tpu as pltpu

F32 = jnp.float32
BF16 = jnp.bfloat16

D_MODEL = 1024
D_GROUP = 256
HEAD_DIM = 64
N_HEADS = 4
N_DIR = 2
DEPTH = 4
GRID_W = 64
POOL_HALF = (1, 2, 4, 8)
MAX_HALF = POOL_HALF[-1]
CONV_W = 31
CONV_PAD = 16
D_FF = 4 * D_MODEL
EPS = 1e-6
RWKV_LN_EPS = 64e-5
CONV_LN_EPS = 1e-5
CHUNK = 64
ROW_TILE = 512
SUB = 8
VMEM_LIMIT = 56 * 1024 * 1024

Z_W = 2944
ZA_W, ZD_QK_W, ZC_W, ZB_W, ZG_W = 1024, 512, 512, 256, 128
ZD_OFF, ZC_OFF, ZB_OFF, ZG_OFF = 1024, 2048, 2560, 2816
ZREST_OFF, ZREST_W = 1536, 1280


def _mm(a, b):
    return jnp.dot(a.astype(BF16), b.astype(BF16), preferred_element_type=F32)


def _split(x, parts):
    out = []
    for _ in range(parts):
        p = x.astype(BF16)
        out.append(p)
        x = x - p.astype(F32)
    return out


def _head_sum(x, ones):
    return sum(jnp.dot(p, ones, preferred_element_type=F32) for p in _split(x, 2))


def _cumsum_mm(tri, x):
    return sum(jnp.dot(tri, p, preferred_element_type=F32) for p in _split(x, 3))


def _tri(n, reverse, strict):
    t = lax.broadcasted_iota(jnp.int32, (n, n), 0)
    s = lax.broadcasted_iota(jnp.int32, (n, n), 1)
    if reverse:
        return (s > t) if strict else (s >= t)
    return (s < t) if strict else (s <= t)


def _head_ones():
    a = lax.broadcasted_iota(jnp.int32, (D_GROUP, D_GROUP), 0) // HEAD_DIM
    b = lax.broadcasted_iota(jnp.int32, (D_GROUP, D_GROUP), 1) // HEAD_DIM
    return (a == b).astype(BF16)


def _sigmoid(x):
    return jax.nn.sigmoid(x)


def _softplus(x):
    return jnp.maximum(x, 0.0) + jnp.log1p(jnp.exp(-jnp.abs(x)))


def _rms(x, g):
    return x * lax.rsqrt(jnp.mean(x * x, axis=-1, keepdims=True) + EPS) * g


def _params(*sem):
    return pltpu.CompilerParams(dimension_semantics=sem, vmem_limit_bytes=VMEM_LIMIT)


def _const(shape):
    return pl.BlockSpec(shape, lambda *_: (0,) * len(shape), pipeline_mode=pl.Buffered(1))


def _halo_specs(width, col, n_rows, tt):
    per = tt // SUB
    last = n_rows // SUB - 1
    prev = pl.BlockSpec((SUB, width), lambda i: (jnp.maximum(i * per - 1, 0), col))
    nxt = pl.BlockSpec((SUB, width), lambda i: (jnp.minimum((i + 1) * per, last), col))
    return prev, nxt


def _mod_kernel(c_ref, w_ref, b_ref, o_ref):
    c = c_ref[...]
    o_ref[0] = _mm(c * _sigmoid(c), w_ref[0]) + b_ref[0]


def modulation(cvec, w_mod, b_mod):
    tn = 1536
    n = w_mod.shape[-1]
    return pl.pallas_call(
        _mod_kernel,
        grid=(DEPTH, n // tn),
        in_specs=[pl.BlockSpec((SUB, D_MODEL), lambda l, j: (0, 0)),
                  pl.BlockSpec((1, D_MODEL, tn), lambda l, j: (l, 0, j)),
                  pl.BlockSpec((1, 1, tn), lambda l, j: (l, 0, j))],
        out_specs=pl.BlockSpec((1, SUB, tn), lambda l, j: (l, 0, j)),
        out_shape=jax.ShapeDtypeStruct((DEPTH, SUB, n), F32),
        compiler_params=_params("parallel", "parallel"),
        name="modulation",
    )(cvec, w_mod, b_mod.reshape(DEPTH, 1, n))


def _mod_spec(per_batch, rows_per_seq):
    if per_batch:
        return pl.BlockSpec((1, 6, D_MODEL), lambda i: (i * ROW_TILE // rows_per_seq, 0, 0))
    return pl.BlockSpec((1, 6, D_MODEL), lambda i: (0, 0, 0))


def _shifted(ze, i, t):
    prev = ze[SUB - 1:SUB - 1 + ROW_TILE, :]
    nxt = ze[SUB + 1:SUB + 1 + ROW_TILE, :]
    pos = (i * ROW_TILE + lax.broadcasted_iota(jnp.int32, (ROW_TILE, 1), 0)) % t
    return jnp.where(pos == 0, 0.0, prev), jnp.where(pos == t - 1, 0.0, nxt)


def _in_proj_kernel(t, x_ref, xp_ref, xn_ref, mod_ref, g_ref, w_ref,
                    mu_ref, kk_ref, ka_ref, rk_ref, w0_ref, a0_ref, wup_ref, aup_ref, gup_ref, cw_ref, gb_ref,
                    z_o, r_o, kk_o, v_o, g_o, bon_o, lw0_o, lw1_o, kd0_o, kd1_o, b0_o, b1_o, q_o, k_o, gate_o):
    i = pl.program_id(0)
    xe = jnp.concatenate([xp_ref[...], x_ref[...], xn_ref[...]], axis=0)
    h = _rms(xe, g_ref[...]) * (1.0 + mod_ref[0, 1:2, :]) + mod_ref[0, 0:1, :]
    ze = jnp.dot(h.astype(BF16), w_ref[...], preferred_element_type=F32)
    mid = slice(SUB, SUB + ROW_TILE)
    z_o[...] = ze[mid, ZREST_OFF:ZREST_OFF + ZREST_W]

    za = ze[:, 0:ZA_W]
    z = za[mid, :]
    prev, nxt = _shifted(za, i, t)
    zm = z + (0.5 * (prev + nxt) - z) * mu_ref[...]
    r, k, v = zm[:, 0:256], zm[:, 256:512], zm[:, 512:768]
    lora = zm[:, 768:896]
    gd = zm[:, 896:1024]
    ones = _head_ones()
    kk = k * kk_ref[...]
    kk = kk * lax.rsqrt(_head_sum(kk * kk, ones) + 1e-12)
    wt = jnp.tanh(lora)
    r_o[...] = r
    kk_o[...] = kk
    v_o[...] = v
    g_o[...] = _mm(_sigmoid(gd), gup_ref[...])
    bon = jnp.zeros_like(r)
    for d, (lw_o, kd_o, b_o) in enumerate(((lw0_o, kd0_o, b0_o), (lw1_o, kd1_o, b1_o))):
        lw_o[...] = -math.exp(-0.5) * _sigmoid(w0_ref[d] + _mm(wt, wup_ref[d]))
        a = _sigmoid(a0_ref[d] + _mm(lora, aup_ref[d]))
        kd = k * (1.0 + (a - 1.0) * ka_ref[...])
        kd_o[...] = kd
        b_o[...] = -a * kk
        bon = bon + _head_sum(r * kd * rk_ref[...], ones) * v
    bon_o[...] = bon

    zq = ze[:, ZD_OFF:ZD_OFF + ZD_QK_W]
    prev, nxt = _shifted(zq, i, t)
    qk = cw_ref[0:1, :] * prev + cw_ref[1:2, :] * zq[mid, :] + cw_ref[2:3, :] * nxt
    qk = qk * _sigmoid(qk)
    q_o[...] = qk[:, :D_GROUP]
    k_o[...] = qk[:, D_GROUP:] * (1.0 / math.sqrt(HEAD_DIM))
    gb = ze[mid, ZG_OFF:ZG_OFF + ZG_W] + gb_ref[...]
    lane = lax.broadcasted_iota(jnp.int32, gb.shape, 1)
    gate_o[...] = jnp.where(lane < N_DIR * N_HEADS, gb, -_softplus(-gb))


def in_proj(x, mod, p, t):
    n = x.shape[0]
    tile = pl.BlockSpec((ROW_TILE, D_MODEL), lambda i: (i, 0))
    prev, nxt = _halo_specs(D_MODEL, 0, n, ROW_TILE)
    out = pl.BlockSpec((ROW_TILE, D_GROUP), lambda i: (i, 0))
    vec = _const((1, D_GROUP))
    rw, ml = p['rwkv'], p['mlstm']
    tok = jax.ShapeDtypeStruct((n, D_GROUP), F32)
    return pl.pallas_call(
        functools.partial(_in_proj_kernel, t),
        grid=(n // ROW_TILE,),
        in_specs=[tile, prev, nxt, _mod_spec(mod.shape[0] > 1, t), _const((1, D_MODEL)), _const((D_MODEL, Z_W)),
                  _const((1, ZA_W)), vec, vec, vec,
                  _const((N_DIR, 1, D_GROUP)), _const((N_DIR, 1, D_GROUP)),
                  _const((N_DIR, 128, D_GROUP)), _const((N_DIR, 128, D_GROUP)), _const((128, D_GROUP)),
                  _const((3, ZD_QK_W)), _const((1, ZG_W))],
        out_specs=[pl.BlockSpec((ROW_TILE, ZREST_W), lambda i: (i, 0))] + [out] * 13
                  + [pl.BlockSpec((ROW_TILE, ZG_W), lambda i: (i, 0))],
        out_shape=[jax.ShapeDtypeStruct((n, ZREST_W), F32)] + [tok] * 13 + [jax.ShapeDtypeStruct((n, ZG_W), F32)],
        compiler_params=_params("parallel"),
        name="in_proj",
    )(x, x, x, mod, p['norm1_g'], p['w_in'], rw['mu'], rw['k_k'], rw['k_a'], rw['r_k'], rw['w0'], rw['a0'],
      rw['w_up'], rw['a_up'], rw['g_up'], ml['qk_conv'], ml['gate_bias'])


SCAN_CB = 4


def _bf(x):
    return x.astype(BF16)


def _dot(a, b):
    return jnp.dot(a, b, preferred_element_type=F32)


def _dot_nt(a, b):
    return lax.dot_general(a, b, (((1,), (1,)), ((), ())), preferred_element_type=F32)


def _dot_tn(a, b):
    return lax.dot_general(a, b, (((0,), (0,)), ((), ())), preferred_element_type=F32)


def _tri2(reverse, strict):
    t = lax.broadcasted_iota(jnp.int32, (CHUNK, 2 * CHUNK), 0)
    s = lax.broadcasted_iota(jnp.int32, (CHUNK, 2 * CHUNK), 1) % CHUNK
    if reverse:
        return (s > t) if strict else (s >= t)
    return (s < t) if strict else (s <= t)


def _chunk_rows(c, cb, reverse):
    return slice((cb - 1 - c) * CHUNK, (cb - c) * CHUNK) if reverse else slice(c * CHUNK, (c + 1) * CHUNK)


def _rwkv_scan_kernel(has_init, cb, *refs):
    if has_init:
        s0_ref, refs = refs[0], refs[1:]
    ins, (y0_ref, y1_ref, sfin_ref, s_ref) = refs[:12], refs[12:]
    L = CHUNK
    i = pl.program_id(1)

    @pl.when(i == 0)
    def _():
        if has_init:
            s_ref[...] = s0_ref[0]
        else:
            s_ref[...] = jnp.zeros_like(s_ref)

    eye = (lax.broadcasted_iota(jnp.int32, (L, L), 0) == lax.broadcasted_iota(jnp.int32, (L, L), 1)).astype(F32)
    heads = [slice(h * HEAD_DIM, (h + 1) * HEAD_DIM) for h in range(N_HEADS)]

    blocks = []
    for d in range(N_DIR):
        reverse = d == 1
        incl = _tri(L, reverse, False)
        for c in range(cb):
            rows = _chunk_rows(c, cb, reverse)
            blocks.append(dict(d=d, c=c, rows=rows, reverse=reverse, incl=incl, lw=ins[6 * d + 3][rows, :]))
    for u in blocks:
        u['cum'] = _cumsum_mm(u['incl'].astype(BF16), u['lw'])
    inst = []
    for u in blocks:
        d, rows, cum, lw = u['d'], u['rows'], u['cum'], u['lw']
        r_ref, kk_ref, v_ref, _, kd_ref, b_ref = ins[6 * d:6 * d + 6]
        r, kk, v, kd, b = (ref[rows, :] for ref in (r_ref, kk_ref, v_ref, kd_ref, b_ref))
        last = cum[0:1, :] if u['reverse'] else cum[L - 1:L, :]
        e_neg = jnp.exp(-cum)
        e_end = jnp.exp(last - cum)
        e_all = jnp.exp(last)
        aq = _bf(kk * jnp.exp(cum - lw))
        rq = r * jnp.exp(cum)
        bk = _bf(b * e_neg)
        kq = _bf(kd * e_neg)
        k_end = _bf(kd * e_end)
        b_end = _bf(b * e_end)
        rq_b, v_b = _bf(rq), _bf(v)
        for h, hs in enumerate(heads):
            inst.append(dict(d=d, c=u['c'], h=h, reverse=u['reverse'], aq=aq[:, hs],
                             rq=rq[:, hs], rq_b=rq_b[:, hs], bk=bk[:, hs], kq=kq[:, hs], k_end=k_end[:, hs],
                             b_end=b_end[:, hs], v=v_b[:, hs], e_all=e_all[:, hs]))

    zeros_h = jnp.zeros((L, HEAD_DIM), BF16)
    for t in inst:
        res = _dot_nt(jnp.concatenate([t['aq'], t['rq_b']], axis=0), jnp.concatenate([t['bk'], t['kq']], axis=0))
        t['mn'] = jnp.where(_tri2(t['reverse'], True), res[:L], 0.0)
        t['pp'] = _bf(jnp.where(_tri2(t['reverse'], False), res[L:], 0.0))
        t['m'] = t['mn'][:, :L]
    row = lax.broadcasted_iota(jnp.int32, (L, L), 0)
    col = lax.broadcasted_iota(jnp.int32, (L, L), 1)
    sizes = [2 ** e for e in range(int(math.log2(L)))]
    corner = {s: jnp.logical_and(row // (2 * s) == col // (2 * s), row // s != col // s) for s in sizes}
    for t in inst:
        t['y'] = eye + jnp.where(corner[1], t['m'], 0.0)
    for s in sizes[1:]:
        for t in inst:
            t['my'] = _bf(_dot(_bf(jnp.where(corner[s], t['m'], 0.0)), _bf(t['y'])))
        for t in inst:
            t['y'] = t['y'] + _dot(_bf(t['y']), t['my'])
    for t in inst:
        t['y'] = _bf(t['y'])
        t['nv'] = _bf(_dot(_bf(t['mn']), jnp.concatenate([zeros_h, t['v']], axis=0)))
    for t in inst:
        t['wq'] = _bf(_dot(t['y'], jnp.concatenate([t['nv'], t['aq']], axis=1)))
    for t in inst:
        x = _dot(t['pp'], jnp.concatenate([t['wq'], jnp.concatenate([t['v'], zeros_h], axis=1)], axis=0))
        t['y_loc'] = x[:, :L]
        t['r2'] = _bf(t['rq'] + x[:, L:])
        wb = _dot_tn(t['wq'], t['b_end'])
        t['tq'] = _bf(wb[L:, :])
        t['g'] = _dot_tn(t['v'], t['k_end']) + wb[:L, :]

    state = [[s_ref[d, h] for h in range(N_HEADS)] for d in range(N_DIR)]
    outs = {}
    for c in range(cb):
        now = [t for t in inst if t['c'] == c]
        for t in now:
            s0 = state[t['d']][t['h']]
            s0_b = _bf(s0)
            outs[(t['d'], c, t['h'])] = t['y_loc'] + _dot_nt(t['r2'], s0_b)
            t['s1'] = s0 * t['e_all'] + _dot(s0_b, t['tq']) + t['g']
        for t in now:
            state[t['d']][t['h']] = t['s1']
    for d, y_ref in enumerate((y0_ref, y1_ref)):
        for c in range(cb):
            y_ref[_chunk_rows(c, cb, d == 1), :] = jnp.concatenate([outs[(d, c, h)] for h in range(N_HEADS)], axis=1)
    for d in range(N_DIR):
        for h in range(N_HEADS):
            s_ref[d, h] = state[d][h]

    @pl.when(i == pl.num_programs(1) - 1)
    def _():
        sfin_ref[0] = s_ref[...]


def rwkv_scan(r, kk, v, lw, kd, b, s0, bsz, t):
    cb = SCAN_CB
    blk = cb * CHUNK
    assert t % blk == 0
    nc = t // blk
    fwd = pl.BlockSpec((blk, D_GROUP), lambda bi, i: (bi * nc + i, 0))
    bwd = pl.BlockSpec((blk, D_GROUP), lambda bi, i: (bi * nc + nc - 1 - i, 0))
    st = pl.BlockSpec((1, N_DIR, N_HEADS, HEAD_DIM, HEAD_DIM), lambda bi, i: (bi, 0, 0, 0, 0))
    has_init = s0 is not None
    args = ([s0] if has_init else []) + [r, kk, v, lw[0], kd[0], b[0], r, kk, v, lw[1], kd[1], b[1]]
    in_specs = ([st] if has_init else []) + [fwd] * 6 + [bwd] * 6
    tok = jax.ShapeDtypeStruct((bsz * t, D_GROUP), F32)
    return pl.pallas_call(
        functools.partial(_rwkv_scan_kernel, has_init, cb),
        grid=(bsz, nc),
        in_specs=in_specs,
        out_specs=[fwd, bwd, st],
        out_shape=[tok, tok, jax.ShapeDtypeStruct((bsz, N_DIR, N_HEADS, HEAD_DIM, HEAD_DIM), F32)],
        scratch_shapes=[pltpu.VMEM((N_DIR, N_HEADS, HEAD_DIM, HEAD_DIM), F32)],
        compiler_params=_params("parallel", "arbitrary"),
        name="rwkv_scan",
    )(*args)


def _shift_rows(x, k, pos, n):
    rows = x.shape[0]
    rolled = pltpu.roll(x, k % rows, axis=0)
    ok = (pos >= k) if k > 0 else (pos < n + k)
    return jnp.where(ok, rolled, 0.0)


def _window_sums(x, pos, n):
    fw = x
    bw = _shift_rows(x, 1, pos, n)
    out = [fw + bw]
    for h in POOL_HALF[:-1]:
        fw = fw + _shift_rows(fw, -h, pos, n)
        bw = bw + _shift_rows(bw, h, pos, n)
        out.append(fw + bw)
    return out


def _by_group(vals, lane):
    out = vals[-1]
    for g in range(len(vals) - 2, -1, -1):
        out = jnp.where(lane < (g + 1) * HEAD_DIM, vals[g], out)
    return out


def _clipped_count(pos, h, n):
    return jnp.minimum(pos + h, n) - jnp.maximum(pos - h, 0)


POOL_SEQ_GROUP = 4


def _pool_seq_kernel(t, x_ref, o_ref):
    x = x_ref[...]
    pos = lax.broadcasted_iota(jnp.int32, x.shape, 0) % t
    lane = lax.broadcasted_iota(jnp.int32, x.shape, 1)
    sums = _by_group(_window_sums(x, pos, t), lane)
    cnt = _by_group([_clipped_count(pos, h, t) for h in POOL_HALF], lane).astype(F32)
    o_ref[...] = sums / cnt - x


def _pool_grid_kernel(x_ref, o_ref, cs_ref):
    gw = GRID_W
    rows = x_ref.shape[0] // gw
    shape = (gw, D_GROUP)
    col = lax.broadcasted_iota(jnp.int32, shape, 0)
    lane = lax.broadcasted_iota(jnp.int32, shape, 1)
    half = _by_group([jnp.full(shape, h, jnp.int32) for h in POOL_HALF], lane)
    cnt_c = _by_group([_clipped_count(col, h, gw) for h in POOL_HALF], lane)

    def col_stage(r, _):
        at = pl.ds(pl.multiple_of(r * gw, gw), gw)
        cs_ref[at, :] = _by_group(_window_sums(x_ref[at, :], col, gw), lane)
        return 0

    lax.fori_loop(0, rows, col_stage, 0)

    def row_stage(r, _):
        at = pl.ds(pl.multiple_of(r * gw, gw), gw)
        acc = jnp.zeros(shape, F32)
        for o in range(-MAX_HALF, MAX_HALF):
            rr = r + o
            src = pl.ds(pl.multiple_of(jnp.clip(rr, 0, rows - 1) * gw, gw), gw)
            in_window = (half > o) if o >= 0 else (half >= -o)
            ok = jnp.logical_and(in_window, jnp.logical_and(rr >= 0, rr < rows))
            acc = acc + jnp.where(ok, cs_ref[src, :], 0.0)
        cnt = (_clipped_count(r, half, rows) * cnt_c).astype(F32)
        o_ref[at, :] = acc / cnt - x_ref[at, :]
        return 0

    lax.fori_loop(0, rows, row_stage, 0)


def pool_mix(z, bsz, t, grid):
    n = bsz * t
    col = (ZB_OFF - ZREST_OFF) // ZB_W
    scratch = [pltpu.VMEM((t, D_GROUP), F32)] if grid else []
    g = 1 if grid else POOL_SEQ_GROUP
    return pl.pallas_call(
        _pool_grid_kernel if grid else functools.partial(_pool_seq_kernel, t),
        grid=(bsz // g,),
        in_specs=[pl.BlockSpec((g * t, ZB_W), lambda i: (i, col))],
        out_specs=pl.BlockSpec((g * t, D_GROUP), lambda i: (i, 0)),
        out_shape=jax.ShapeDtypeStruct((n, D_GROUP), F32),
        scratch_shapes=scratch,
        compiler_params=_params("parallel"),
        name="pool_grid" if grid else "pool_seq",
    )(z)


CONV_TILE = 64


def _conv_kernel(z_ref, dw_ref, b_ref, o_ref, pad_ref):
    t = z_ref.shape[0]
    rt = CONV_TILE
    edge = jnp.zeros((CONV_PAD, D_GROUP), F32)
    pad_ref[0:CONV_PAD, :] = edge
    pad_ref[t + CONV_PAD:t + 2 * CONV_PAD, :] = edge

    def glu(i, _):
        base = pl.multiple_of(i * rt, rt)
        zz = z_ref[pl.ds(base, rt), :]
        pad_ref[pl.ds(base + CONV_PAD, rt), :] = zz[:, :D_GROUP] * _sigmoid(zz[:, D_GROUP:])
        return 0

    lax.fori_loop(0, t // rt, glu, 0)

    def tile(i, _):
        base = pl.multiple_of(i * rt, rt)
        rows = rt + 2 * CONV_PAD
        first = CONV_PAD - CONV_W // 2
        halves = []
        for lanes in (slice(0, D_GROUP // 2), slice(D_GROUP // 2, D_GROUP)):
            acc = jnp.broadcast_to(b_ref[:, lanes], (rt, D_GROUP // 2))
            win = pad_ref[pl.ds(base, rows), lanes]
            for sub in range(SUB):
                shifted = pltpu.roll(win, (rows - sub) % rows, axis=0) if sub else win
                for j in range(CONV_W):
                    off = first + j
                    if off % SUB == sub:
                        acc = acc + dw_ref[j:j + 1, lanes] * shifted[off - sub:off - sub + rt, :]
            halves.append(acc)
        o_ref[pl.ds(base, rt), :] = jnp.concatenate(halves, axis=1)
        return 0

    lax.fori_loop(0, t // rt, tile, 0)


def conv_mix(z, p, bsz, t):
    n = bsz * t
    col = (ZC_OFF - ZREST_OFF) // ZC_W
    vec = _const((1, D_GROUP))
    return pl.pallas_call(
        _conv_kernel,
        grid=(bsz,),
        in_specs=[pl.BlockSpec((t, ZC_W), lambda i: (i, col)), _const((CONV_W, D_GROUP)), vec],
        out_specs=pl.BlockSpec((t, D_GROUP), lambda i: (i, 0)),
        out_shape=jax.ShapeDtypeStruct((n, D_GROUP), F32),
        scratch_shapes=[pltpu.VMEM((t + 2 * CONV_PAD, D_GROUP), F32)],
        compiler_params=_params("parallel"),
        name="conv_mix",
    )(z, p['dw'], p['b'])


def _lane_group(n_lanes):
    return lax.broadcasted_iota(jnp.int32, (1, n_lanes), 1) // HEAD_DIM


def _mlstm_scan_kernel(has_init, cb, *refs):
    if has_init:
        c0_ref, n0_ref, m0_ref = refs[:3]
        refs = refs[3:]
    ins, (h0_ref, h1_ref, cfin_ref, nfin_ref, mfin_ref, c_ref, n_ref, m_ref) = refs[:8], refs[8:]
    L = CHUNK
    nh = N_DIR * N_HEADS
    W = D_GROUP
    i = pl.program_id(1)

    @pl.when(i == 0)
    def _():
        c_ref[...] = jnp.zeros_like(c_ref)
        if has_init:
            for d in range(N_DIR):
                for h in range(N_HEADS):
                    hs = slice(h * HEAD_DIM, (h + 1) * HEAD_DIM)
                    c_ref[d, hs, hs] = c0_ref[0, d, h]
            n_ref[...] = n0_ref[0]
            m_ref[...] = m0_ref[0]
        else:
            n_ref[...] = jnp.zeros_like(n_ref)
            m_ref[...] = jnp.zeros_like(m_ref)

    ones = _head_ones()
    row_head = lax.broadcasted_iota(jnp.int32, (W, 1), 0) // HEAD_DIM
    same_head = row_head == _lane_group(W)
    gate_lane = lax.broadcasted_iota(jnp.int32, (ZG_W, 1), 0)
    pick = (lax.broadcasted_iota(jnp.int32, (nh, ZG_W), 0) == lax.broadcasted_iota(jnp.int32, (nh, ZG_W), 1)).astype(BF16)
    s_pos = lax.broadcasted_iota(jnp.int32, (L, W), 1) % L
    t_pos = lax.broadcasted_iota(jnp.int32, (L, W), 0)
    lane128 = lax.broadcasted_iota(jnp.int32, (L, 128), 1)

    blocks = []
    for d in range(N_DIR):
        reverse = d == 1
        q_ref, k_ref, v_ref, g_ref = ins[4 * d:4 * d + 4]
        spread_i = (gate_lane == d * N_HEADS + _lane_group(W)).astype(BF16)
        spread_b = (gate_lane == nh + d * N_HEADS + _lane_group(W)).astype(BF16)
        for c in range(cb):
            rows = _chunk_rows(c, cb, reverse)
            blocks.append(dict(d=d, c=c, reverse=reverse, last=0 if reverse else L - 1,
                               causal=(s_pos >= t_pos) if reverse else (s_pos <= t_pos),
                               tri=_tri(L, reverse, False).astype(BF16), spread_i=spread_i, spread_b=spread_b,
                               q=_bf(q_ref[rows, :]), k=k_ref[rows, :], v=_bf(v_ref[rows, :]), g=g_ref[rows, :]))
    for u in blocks:
        u['bc_all'] = _cumsum_mm(u['tri'], u['g'])
    for u in blocks:
        u['src'] = u['g'] - pltpu.roll(u['bc_all'], ZG_W - nh, axis=1)
        u['bb'] = sum(_dot(p, u['spread_b']) for p in _split(u['bc_all'], 3))
        u['li'] = sum(_dot(p, u['spread_i']) for p in _split(u['g'], 3))
    for u in blocks:
        rows8 = sum(_dot_nt(pick, p) for p in _split(u['src'], 3))
        j0 = u['d'] * N_HEADS
        u['src_row'] = jnp.concatenate([rows8[j0 + h:j0 + h + 1, :] for h in range(N_HEADS)], axis=1)
        last = u['last']
        u['b_last'] = u['bb'][last:last + 1, :]
        k4 = jnp.concatenate([_bf(u['k'])] * N_HEADS, axis=0)
        v4 = jnp.concatenate([u['v']] * N_HEADS, axis=0)
        u['k_exp'] = jnp.where(same_head, k4, jnp.zeros_like(k4))
        u['v_exp'] = jnp.where(same_head, v4, jnp.zeros_like(v4))
    for u in blocks:
        u['log_w'] = jnp.where(u['causal'], u['bb'] + u['src_row'], -jnp.inf)
        u['qk'] = _dot_nt(u['q'], u['k_exp'])
        u['gl'] = u['b_last'] - u['bb'] + u['li']
    for u in blocks:
        cols = []
        for half in range(2):
            x = u['log_w'][:, half * 128:(half + 1) * 128]
            lo = jnp.max(jnp.where(lane128 < HEAD_DIM, x, -jnp.inf), axis=1, keepdims=True)
            hi = jnp.max(jnp.where(lane128 >= HEAD_DIM, x, -jnp.inf), axis=1, keepdims=True)
            cols.append(jnp.where(lane128 < HEAD_DIM, lo, hi))
        u['a'] = jnp.concatenate(cols, axis=1)
    for u in blocks:
        last = u['last']
        u['a_last'] = u['a'][last:last + 1, :]
        u['s_loc'] = u['qk'] * jnp.exp(u['log_w'] - u['a'])
    for u in blocks:
        u['wk'] = jnp.exp(u['gl'] - u['a_last']) * u['k']
        u['den_loc'] = _head_sum(u['s_loc'], ones)
        u['num_loc'] = _dot(_bf(u['s_loc']), u['v_exp'])
    for u in blocks:
        u['c_loc'] = jnp.where(same_head, _dot_tn(u['v'], _bf(u['wk'])), 0.0)
        u['n_loc'] = jnp.sum(u['wk'], axis=0, keepdims=True)

    cs = [c_ref[d] for d in range(N_DIR)]
    ns = [n_ref[d:d + 1, :] for d in range(N_DIR)]
    ms = [m_ref[d:d + 1, :] for d in range(N_DIR)]
    for c in range(cb):
        now = [u for u in blocks if u['c'] == c]
        for u in now:
            d = u['d']
            u['c_in'], u['n_in'], u['m_in'] = cs[d], ns[d], ms[d]
            u['m_new'] = jnp.maximum(u['b_last'] + ms[d], u['a_last'])
        for u in now:
            u['carry'] = jnp.exp(u['b_last'] + u['m_in'] - u['m_new'])
            u['fresh'] = jnp.exp(u['a_last'] - u['m_new'])
        for u in now:
            d = u['d']
            cs[d] = u['carry'] * cs[d] + u['fresh'] * u['c_loc']
            ns[d] = u['carry'] * ns[d] + u['fresh'] * u['n_loc']
            ms[d] = u['m_new']

    for u in blocks:
        u['qc'] = _dot_nt(u['q'], _bf(u['c_in']))
        u['qn'] = _head_sum(u['q'].astype(F32) * u['n_in'], ones)
        u['m_t'] = jnp.maximum(u['bb'] + u['m_in'], u['a'])
    for u in blocks:
        u['scale'] = jnp.exp(u['a'] - u['m_t'])
        u['inter'] = jnp.exp(u['bb'] + u['m_in'] - u['m_t'])
        u['floor'] = jnp.exp(-u['m_t'])
    for u in blocks:
        u['num'] = u['scale'] * u['num_loc'] + u['inter'] * u['qc']
        u['den'] = jnp.maximum(jnp.abs(u['scale'] * u['den_loc'] + u['inter'] * u['qn']), u['floor'])
    for u in blocks:
        h_ref = h1_ref if u['reverse'] else h0_ref
        h_ref[_chunk_rows(u['c'], cb, u['reverse']), :] = u['num'] / u['den']
    for d in range(N_DIR):
        c_ref[d] = cs[d]
        n_ref[d:d + 1, :] = ns[d]
        m_ref[d:d + 1, :] = ms[d]

    @pl.when(i == pl.num_programs(1) - 1)
    def _():
        for d in range(N_DIR):
            for h in range(N_HEADS):
                hs = slice(h * HEAD_DIM, (h + 1) * HEAD_DIM)
                cfin_ref[0, d, h] = c_ref[d, hs, hs]
        nfin_ref[0] = n_ref[...]
        mfin_ref[0] = m_ref[...]


def mlstm_scan(q, k, z, g, init, bsz, t):
    cb = SCAN_CB
    blk = cb * CHUNK
    assert t % blk == 0
    nc = t // blk
    vcol = (ZD_OFF + 2 * D_GROUP - ZREST_OFF) // D_GROUP

    def specs(row):
        return [pl.BlockSpec((blk, D_GROUP), lambda bi, i: (row(bi, i), 0)),
                pl.BlockSpec((blk, D_GROUP), lambda bi, i: (row(bi, i), 0)),
                pl.BlockSpec((blk, D_GROUP), lambda bi, i: (row(bi, i), vcol)),
                pl.BlockSpec((blk, ZG_W), lambda bi, i: (row(bi, i), 0))]

    fwd = specs(lambda bi, i: bi * nc + i)
    bwd = specs(lambda bi, i: bi * nc + nc - 1 - i)
    cst = pl.BlockSpec((1, N_DIR, N_HEADS, HEAD_DIM, HEAD_DIM), lambda bi, i: (bi, 0, 0, 0, 0))
    vst = pl.BlockSpec((1, N_DIR, D_GROUP), lambda bi, i: (bi, 0, 0))
    has_init = init is not None
    args = (list(init) if has_init else []) + [q, k, z, g] * 2
    tok = jax.ShapeDtypeStruct((bsz * t, D_GROUP), F32)
    vec = jax.ShapeDtypeStruct((bsz, N_DIR, D_GROUP), F32)
    return pl.pallas_call(
        functools.partial(_mlstm_scan_kernel, has_init, cb),
        grid=(bsz, nc),
        in_specs=([cst, vst, vst] if has_init else []) + fwd + bwd,
        out_specs=[fwd[0], bwd[0], cst, vst, vst],
        out_shape=[tok, tok, jax.ShapeDtypeStruct((bsz, N_DIR, N_HEADS, HEAD_DIM, HEAD_DIM), F32), vec, vec],
        scratch_shapes=[pltpu.VMEM((N_DIR, D_GROUP, D_GROUP), F32),
                        pltpu.VMEM((N_DIR, D_GROUP), F32),
                        pltpu.VMEM((N_DIR, D_GROUP), F32)],
        compiler_params=_params("parallel", "arbitrary"),
        name="mlstm_scan",
    )(*args)


def _mix_out(x_ref, mod_ref, y0_ref, y1_ref, bon_ref, g_ref, lng_ref, lnb_ref, pd_ref, pw_ref, ps_ref,
             cv_ref, clg_ref, clb_ref, cpw_ref, h0_ref, h1_ref, o_ref, hng_ref, w_ref):
    ones = _head_ones()
    inv = 1.0 / HEAD_DIM
    y = y0_ref[...] + y1_ref[...] + bon_ref[...]
    mu = _head_sum(y, ones) * inv
    yc = y - mu
    var = _head_sum(yc * yc, ones) * inv
    ya = (yc * lax.rsqrt(var + RWKV_LN_EPS) * lng_ref[...] + lnb_ref[...]) * g_ref[...]
    y_pool = _mm(pd_ref[...], pw_ref[...]) * ps_ref[...]
    cv = cv_ref[...]
    cmu = jnp.mean(cv, axis=-1, keepdims=True)
    cvar = jnp.mean(jnp.square(cv - cmu), axis=-1, keepdims=True)
    cn = (cv - cmu) * lax.rsqrt(cvar + CONV_LN_EPS) * clg_ref[...] + clb_ref[...]
    y_conv = _mm(cn * _sigmoid(cn), cpw_ref[...])
    hs = (h0_ref[...] + h1_ref[...]) * _sigmoid(o_ref[...])
    yd = hs * lax.rsqrt(_head_sum(hs * hs, ones) * inv + EPS) * hng_ref[...]
    g = D_GROUP
    mixed = (_mm(ya, w_ref[0:g, :]) + _mm(y_pool, w_ref[g:2 * g, :])
             + _mm(y_conv, w_ref[2 * g:3 * g, :]) + _mm(yd, w_ref[3 * g:4 * g, :]))
    return x_ref[...] + mod_ref[0, 2:3, :] * mixed


def _mlp(final, x, mod_ref, g_ref, w1_ref, b1_ref, w2_ref, b2_ref, fg_ref):
    h = _rms(x, g_ref[...]) * (1.0 + mod_ref[0, 4:5, :]) + mod_ref[0, 3:4, :]
    a = jnp.maximum(jnp.dot(h.astype(BF16), w1_ref[...], preferred_element_type=F32) + b1_ref[...], 0.0)
    f = jnp.dot((a * a).astype(BF16), w2_ref[...], preferred_element_type=F32) + b2_ref[...]
    x = x + mod_ref[0, 5:6, :] * f
    return _rms(x, fg_ref[...]) if final else x


def _mix_mlp_kernel(final, *refs):
    mix_refs, mlp_refs, out_ref = refs[:20], refs[20:26], refs[26]
    x = _mix_out(*mix_refs)
    out_ref[...] = _mlp(final, x, mix_refs[1], *mlp_refs)


def mix_mlp(x, mod, ra, pooled, conv, md, z, p, final_g, t, final):
    n = x.shape[0]
    tok = pl.BlockSpec((ROW_TILE, D_GROUP), lambda i: (i, 0))
    ocol = (ZD_OFF + 3 * D_GROUP - ZREST_OFF) // D_GROUP
    vec = _const((1, D_GROUP))
    wide = _const((1, D_MODEL))
    sq = _const((D_GROUP, D_GROUP))
    row = pl.BlockSpec((ROW_TILE, D_MODEL), lambda i: (i, 0))
    return pl.pallas_call(
        functools.partial(_mix_mlp_kernel, final),
        grid=(n // ROW_TILE,),
        in_specs=[row, _mod_spec(mod.shape[0] > 1, t), tok, tok, tok, tok, vec, vec, tok, sq, vec,
                  tok, vec, vec, sq, tok, tok,
                  pl.BlockSpec((ROW_TILE, D_GROUP), lambda i: (i, ocol)), vec, _const((D_MODEL, D_MODEL)),
                  wide, _const((D_MODEL, D_FF)), _const((1, D_FF)), _const((D_FF, D_MODEL)), wide, wide],
        out_specs=row,
        out_shape=jax.ShapeDtypeStruct((n, D_MODEL), F32),
        compiler_params=_params("parallel"),
        name="mix_mlp",
    )(x, mod, ra['y0'], ra['y1'], ra['bon'], ra['g'], p['rwkv_ln_g'], p['rwkv_ln_b'],
      pooled, p['pool_w'], p['pool_scale'], conv, p['conv']['ln_g'], p['conv']['ln_b'], p['conv']['pw'],
      md['h0'], md['h1'], z, p['hn_g'], p['w_out'],
      p['norm2_g'], p['w1'], p['b1'], p['w2'], p['b2'], final_g)


def _trunk_layer(x, mod, p, states, bsz, t, grid, final_g, final):
    z, r, kk, v, g, bon, lw0, lw1, kd0, kd1, b0, b1, q, k, gates = in_proj(x, mod, p, t)
    y0, y1, s_rwkv = rwkv_scan(r, kk, v, (lw0, lw1), (kd0, kd1), (b0, b1),
                               None if states is None else states[0], bsz, t)
    yb = pool_mix(z, bsz, t, grid)
    yc = conv_mix(z, p['conv'], bsz, t)
    h0, h1, s_c, s_n, s_m = mlstm_scan(q, k, z, gates, None if states is None else states[1:], bsz, t)
    x = mix_mlp(x, mod, dict(y0=y0, y1=y1, bon=bon, g=g), yb, yc, dict(h0=h0, h1=h1), z, p, final_g, t, final)
    return x, (s_rwkv, s_c, s_n, s_m)


def _layer_params(l, norm1_g, norm2_g, w_in, w_out, rwkv_mu, rwkv_w0, rwkv_w_up, rwkv_a0, rwkv_a_up, rwkv_g_up,
                  rwkv_k_k, rwkv_k_a, rwkv_r_k, rwkv_ln_g, rwkv_ln_b, pool_w, pool_scale, conv_dw, conv_b,
                  conv_ln_g, conv_ln_b, conv_pw, mlstm_qk_conv, mlstm_i_bias, mlstm_f_bias, mlstm_hn_g,
                  mlp_w1, mlp_b1, mlp_w2, mlp_b2):
    row = lambda a: a[l].reshape(1, -1)
    w = w_in[l]
    a_w, b_w, c_w = 1024, 256, 512
    wa, wb, wc, wd = w[:, :a_w], w[:, a_w:a_w + b_w], w[:, a_w + b_w:a_w + b_w + c_w], w[:, a_w + b_w + c_w:]
    nh = N_DIR * N_HEADS
    qkv, gates, o = wd[:, :3 * D_GROUP], wd[:, 3 * D_GROUP:3 * D_GROUP + 2 * nh], wd[:, 3 * D_GROUP + 2 * nh:]
    w_cat = jnp.concatenate([wa, qkv, o, wc, wb, gates, jnp.zeros((D_MODEL, ZG_W - 2 * nh), F32)], axis=1)
    rank = rwkv_w_up.shape[2]
    zeros = jnp.zeros((N_DIR, rank, D_GROUP), F32)
    pool_bd = jax.scipy.linalg.block_diag(*[pool_w[l, g] for g in range(len(POOL_HALF))])
    gate_bias = jnp.concatenate([mlstm_i_bias[l], mlstm_f_bias[l], jnp.zeros((ZG_W - 2 * nh,), F32)]).reshape(1, ZG_W)
    return dict(
        norm1_g=row(norm1_g), norm2_g=row(norm2_g), w_in=w_cat.astype(BF16), w_out=w_out[l].astype(BF16),
        rwkv=dict(mu=row(rwkv_mu), k_k=row(rwkv_k_k), k_a=row(rwkv_k_a), r_k=row(rwkv_r_k),
                  w0=rwkv_w0[l].reshape(N_DIR, 1, D_GROUP), a0=rwkv_a0[l].reshape(N_DIR, 1, D_GROUP),
                  w_up=jnp.concatenate([rwkv_w_up[l], zeros], axis=1).astype(BF16),
                  a_up=jnp.concatenate([zeros, rwkv_a_up[l]], axis=1).astype(BF16),
                  g_up=rwkv_g_up[l].astype(BF16)),
        rwkv_ln_g=row(rwkv_ln_g), rwkv_ln_b=row(rwkv_ln_b),
        pool_w=pool_bd.astype(BF16), pool_scale=row(pool_scale),
        conv=dict(dw=conv_dw[l], b=row(conv_b), ln_g=row(conv_ln_g), ln_b=row(conv_ln_b), pw=conv_pw[l].astype(BF16)),
        mlstm=dict(qk_conv=mlstm_qk_conv[l], gate_bias=gate_bias),
        hn_g=row(mlstm_hn_g),
        w1=mlp_w1[l].astype(BF16), b1=row(mlp_b1), w2=mlp_w2[l].astype(BF16), b2=row(mlp_b2),
    )


def kernel(x_prompt, x_sample, c, state_rwkv, state_mlstm_C, state_mlstm_n, state_mlstm_m, c_ctx, w_mod, b_mod, norm1_g, norm2_g, w_in, w_out, rwkv_mu, rwkv_w0, rwkv_w_up, rwkv_a0, rwkv_a_up, rwkv_g_up, rwkv_k_k, rwkv_k_a, rwkv_r_k, rwkv_ln_g, rwkv_ln_b, pool_w, pool_scale, conv_dw, conv_b, conv_ln_g, conv_ln_b, conv_pw, mlstm_qk_conv, mlstm_i_bias, mlstm_f_bias, mlstm_hn_g, mlp_w1, mlp_b1, mlp_w2, mlp_b2, final_g):
    bp, tp, d = x_prompt.shape
    bs, ts, _ = x_sample.shape
    nh = N_DIR * N_HEADS
    cvec = jnp.concatenate([c_ctx[None, :], c, jnp.zeros((SUB - 1 - bs, d), F32)], axis=0)
    mods = modulation(cvec, w_mod, b_mod).reshape(DEPTH, SUB, 6, d)
    xp = x_prompt.reshape(bp * tp, d)
    xs = x_sample.reshape(bs * ts, d)
    fg = final_g.reshape(1, d)
    new = [[], [], [], []]
    for l in range(DEPTH):
        p = _layer_params(l, norm1_g, norm2_g, w_in, w_out, rwkv_mu, rwkv_w0, rwkv_w_up, rwkv_a0, rwkv_a_up,
                          rwkv_g_up, rwkv_k_k, rwkv_k_a, rwkv_r_k, rwkv_ln_g, rwkv_ln_b, pool_w, pool_scale,
                          conv_dw, conv_b, conv_ln_g, conv_ln_b, conv_pw, mlstm_qk_conv, mlstm_i_bias,
                          mlstm_f_bias, mlstm_hn_g, mlp_w1, mlp_b1, mlp_w2, mlp_b2)
        final = l == DEPTH - 1
        xp, st = _trunk_layer(xp, mods[l, 0:1], p, None, bp, tp, False, fg, final)
        for acc, s in zip(new, st):
            acc.append(s)
        cached = (state_rwkv[:, l], state_mlstm_C[:, l],
                  state_mlstm_n[:, l].reshape(bs, N_DIR, D_GROUP),
                  jnp.repeat(state_mlstm_m[:, l], HEAD_DIM, axis=-1))
        xs, _ = _trunk_layer(xs, mods[l, 1:1 + bs], p, cached, bs, ts, True, fg, final)
    return (xp.reshape(bp, tp, d), xs.reshape(bs, ts, d),
            jnp.stack(new[0], axis=1), jnp.stack(new[1], axis=1),
            jnp.stack(new[2], axis=1).reshape(bp, DEPTH, N_DIR, N_HEADS, HEAD_DIM),
            jnp.stack(new[3], axis=1).reshape(bp, DEPTH, N_DIR, N_HEADS, HEAD_DIM)[..., 0])
```
